```python
import math
import jax, jax.numpy as jnp
from jax import lax
import numpy as np

D_MODEL = 1024
BATCH = 8
SEQ = 2048
DEPTH = 4

GRID_W = 64
CTX_LEN = 256
EPS = 1e-6
D_RNN = 1024
LRU_BLOCKS = 16
LRU_BW = D_RNN // LRU_BLOCKS
CONV_W = 4
CONV_LEFT = 2
LRU_C = 8.0
ATTN_HEADS = 8
ATTN_DH = D_MODEL // (2 * ATTN_HEADS)
D_QK = ATTN_HEADS * 2 * ATTN_DH
D_V = ATTN_HEADS * 2 * ATTN_DH
Q_BLOCK = 128
ROPE_THETA = 10000.0
ROPE_FREQS = ATTN_DH // 4
PEER_HEADS = 8
N_KEYS = 128
N_EXPERTS = N_KEYS * N_KEYS
PEER_DQ = 128
PEER_TOPK = 16
PEER_CHUNK = 32
D_IN = 2 * D_RNN + 2 * D_QK + D_V + 2 * D_MODEL
SPLIT_AT = (D_RNN, 2 * D_RNN, 2 * D_RNN + D_QK, 2 * D_RNN + 2 * D_QK,
            2 * D_RNN + 2 * D_QK + D_V, 2 * D_RNN + 2 * D_QK + D_V + D_MODEL)

kernel_name = 'hybrid_rglru_diffattn_peer_dit'


def rmsnorm(x, g):
    xf = x.astype(jnp.float32)
    y = xf * lax.rsqrt(jnp.mean(xf * xf, axis=-1, keepdims=True) + EPS)
    return (y * g.astype(jnp.float32)).astype(x.dtype)


def modulate(h, shift, scale):
    return h * (1.0 + scale) + shift


def dwconv_centred(x, w, b):
    L = x.shape[1]
    xp = jnp.pad(x, ((0, 0), (CONV_LEFT, CONV_W - 1 - CONV_LEFT), (0, 0)))
    y = b
    for k in range(CONV_W):
        y = y + xp[:, k:k + L] * w[k]
    return y


def block_diag_linear(x, w, b):
    B, L, _ = x.shape
    xb = x.reshape(B, L, LRU_BLOCKS, LRU_BW)
    return jnp.einsum('blni,nij->blnj', xb, w).reshape(B, L, D_RNN) + b


def rglru_scan(u, w, b, lam, h0, reverse):
    uf = u.astype(jnp.float32)
    rec = jax.nn.sigmoid(block_diag_linear(u, w[0], b[0]).astype(jnp.float32))
    inp = jax.nn.sigmoid(block_diag_linear(u, w[1], b[1]).astype(jnp.float32))
    log_a = -LRU_C * rec * jax.nn.softplus(-lam.astype(jnp.float32))
    a = jnp.exp(log_a)
    drive = jnp.sqrt(-jnp.expm1(2.0 * log_a)) * (inp * uf)
    edge = -1 if reverse else 0
    drive = drive.at[:, edge].add(a[:, edge] * h0)

    def combine(earlier, later):
        return (earlier[0] * later[0], later[0] * earlier[1] + later[1])

    _, h = lax.associative_scan(combine, (a, drive), reverse=reverse, axis=1)
    return h, (h[:, 0] if reverse else h[:, -1])


def axial_rope_tables(n_tokens):
    rows = n_tokens // GRID_W
    row, col = jnp.meshgrid(jnp.arange(rows), jnp.arange(GRID_W), indexing='ij')
    pos = jnp.stack([row.reshape(-1), col.reshape(-1)], axis=-1).astype(jnp.float32)
    inv = ROPE_THETA ** (-jnp.arange(ROPE_FREQS, dtype=jnp.float32) / ROPE_FREQS)
    ang = pos[:, :, None] * inv
    return jnp.cos(ang), jnp.sin(ang)


def apply_rope(x, cos, sin):
    B, L, H, M, _ = x.shape
    xr = x.astype(jnp.float32).reshape(B, L, H, M, 2, 2, ROPE_FREQS)
    x1, x2 = xr[..., 0, :], xr[..., 1, :]
    c = cos[None, :, None, None]
    s = sin[None, :, None, None]
    out = jnp.stack([x1 * c - x2 * s, x2 * c + x1 * s], axis=-2)
    return out.reshape(x.shape).astype(x.dtype)


def diff_attend(q, k, v, lam):
    B, Lq = q.shape[:2]
    nb = Lq // Q_BLOCK
    qb = q.reshape(B, nb, Q_BLOCK, ATTN_HEADS, 2, ATTN_DH).swapaxes(0, 1)
    vf = v.astype(jnp.float32)

    def one_block(qi):
        s = jnp.einsum('bqhmd,bkhmd->bhmqk', qi, k, preferred_element_type=jnp.float32)
        p = jax.nn.softmax(s, axis=-1)
        w = p[:, :, 0] - lam * p[:, :, 1]
        return jnp.einsum('bhqk,bkhe->bqhe', w, vf)

    o = lax.map(one_block, qb)
    return o.swapaxes(0, 1).reshape(B, Lq, ATTN_HEADS, 2 * ATTN_DH)


def mixer(hx, hc, w_in, conv_w, conv_b, lru_w, lru_b, lru_lam, diff_lam, subln_g,
          w_br_lru, w_br_attn, w_out, lambda_init, cos, sin, update_ctx):
    B, L, _ = hx.shape
    C = hc.shape[1]
    dt = hx.dtype
    ul, gl, ql, kl, vl, gal, gbl = jnp.split(hx @ w_in, SPLIT_AT, axis=-1)
    uc, gc, qc, kc, vc, gac, gbc = jnp.split(hc @ w_in, SPLIT_AT, axis=-1)

    uc = dwconv_centred(uc, conv_w, conv_b)
    ul = dwconv_centred(ul, conv_w, conv_b)
    h0 = jnp.zeros((B, D_RNN), jnp.float32)
    hsum_c = jnp.zeros((B, C, D_RNN), jnp.float32)
    hsum_l = jnp.zeros((B, L, D_RNN), jnp.float32)
    for d, rev in enumerate((False, True)):
        h_c, state_c = rglru_scan(uc, lru_w[d], lru_b[d], lru_lam[d], h0, rev)
        h_l, _ = rglru_scan(ul, lru_w[d], lru_b[d], lru_lam[d], state_c, rev)
        hsum_c = hsum_c + h_c
        hsum_l = hsum_l + h_l

    def lru_branch(hsum, gate):
        return (hsum * jax.nn.gelu(gate.astype(jnp.float32))).astype(dt) @ w_br_lru

    lp = diff_lam.astype(jnp.float32)
    lam = jnp.exp(jnp.sum(lp[0] * lp[1])) - jnp.exp(jnp.sum(lp[2] * lp[3])) + lambda_init
    scale = ATTN_DH ** -0.5
    ql = apply_rope(ql.reshape(B, L, ATTN_HEADS, 2, ATTN_DH), cos, sin) * scale
    kl = apply_rope(kl.reshape(B, L, ATTN_HEADS, 2, ATTN_DH), cos, sin)
    qc = qc.reshape(B, C, ATTN_HEADS, 2, ATTN_DH) * scale
    kc = kc.reshape(B, C, ATTN_HEADS, 2, ATTN_DH)
    vl = vl.reshape(B, L, ATTN_HEADS, 2 * ATTN_DH)
    vc = vc.reshape(B, C, ATTN_HEADS, 2 * ATTN_DH)

    def attn_branch(q, k, v):
        o = rmsnorm(diff_attend(q, k, v, lam), subln_g) * (1.0 - lambda_init)
        return o.reshape(q.shape[0], q.shape[1], D_V).astype(dt) @ w_br_attn

    ya_l = attn_branch(ql, jnp.concatenate([kc, kl], axis=1), jnp.concatenate([vc, vl], axis=1))
    out_l = (jax.nn.sigmoid(gal) * lru_branch(hsum_l, gl) + jax.nn.sigmoid(gbl) * ya_l) @ w_out
    out_c = None
    if update_ctx:
        ya_c = attn_branch(qc, kc, vc)
        out_c = (jax.nn.sigmoid(gac) * lru_branch(hsum_c, gc) + jax.nn.sigmoid(gbc) * ya_c) @ w_out
    return out_l, out_c


def peer_ffn(h, wq, keys, u_tab, v_tab):
    B, L, D = h.shape
    q = (h @ wq).reshape(B, L, PEER_HEADS, 2, PEER_DQ // 2).astype(jnp.float32)
    s = jnp.einsum('blhpd,hpnd->blhpn', q, keys.astype(jnp.float32))
    v_top, i_top = lax.top_k(s, PEER_TOPK)
    cand = v_top[..., 0, :, None] + v_top[..., 1, None, :]
    best, pos = lax.top_k(cand.reshape(B, L, PEER_HEADS, PEER_TOPK * PEER_TOPK), PEER_TOPK)
    i1 = jnp.take_along_axis(i_top[..., 0, :], pos // PEER_TOPK, axis=-1)
    i2 = jnp.take_along_axis(i_top[..., 1, :], pos % PEER_TOPK, axis=-1)
    idx = i1 * N_KEYS + i2
    g = jax.nn.softmax(best, axis=-1)
    nc = L // PEER_CHUNK

    def chunk(a):
        return a.reshape((B, nc, PEER_CHUNK) + a.shape[2:]).swapaxes(0, 1)

    def body(args):
        hc, ic, gc = args
        act = jax.nn.gelu(jnp.einsum('btd,bthkd->bthk', hc, u_tab[ic]).astype(jnp.float32))
        return jnp.einsum('bthk,bthkd->btd', (act * gc).astype(h.dtype), v_tab[ic])

    y = lax.map(body, (chunk(h), chunk(idx), chunk(g)))
    return y.swapaxes(0, 1).reshape(B, L, D)


def setup_inputs(seed: int = 0) -> dict:
    key = jax.random.key(seed)
    ks = jax.random.split(key, 24)

    def nrm(k, shape, s):
        return jax.random.normal(k, shape, jnp.float32) * s

    D = D_MODEL
    u = jax.random.uniform(ks[13], (DEPTH, 2, D_RNN), jnp.float32, minval=0.9, maxval=0.999)
    a = u ** (1.0 / LRU_C)
    return {
        'x': nrm(ks[0], (BATCH, SEQ, D), 1.0),
        'c': nrm(ks[1], (BATCH, D), 1.0),
        'ctx': nrm(ks[2], (BATCH, CTX_LEN, D), 1.0),
        'c_ctx': nrm(ks[3], (D,), 1.0),
        'mod_w': nrm(ks[4], (DEPTH, D, 6 * D), 0.5 * D ** -0.5),
        'mod_b': nrm(ks[5], (DEPTH, 6 * D), 0.01),
        'norm1_g': 1.0 + nrm(ks[6], (DEPTH, D), 0.01),
        'norm2_g': 1.0 + nrm(ks[7], (DEPTH, D), 0.01),
        'w_in': nrm(ks[8], (DEPTH, D, D_IN), D ** -0.5),
        'conv_w': nrm(ks[9], (DEPTH, CONV_W, D_RNN), CONV_W ** -0.5),
        'conv_b': nrm(ks[10], (DEPTH, D_RNN), 0.01),
        'lru_w': nrm(ks[11], (DEPTH, 2, 2, LRU_BLOCKS, LRU_BW, LRU_BW), LRU_BW ** -0.5),
        'lru_b': nrm(ks[12], (DEPTH, 2, 2, D_RNN), 0.01),
        'lru_lam': jnp.log(a) - jnp.log1p(-a),
        'diff_lam': nrm(ks[14], (DEPTH, 4, ATTN_DH), 0.1),
        'subln_g': 1.0 + nrm(ks[15], (DEPTH, 2 * ATTN_DH), 0.01),
        'w_br_lru': nrm(ks[16], (DEPTH, D_RNN, D), D_RNN ** -0.5),
        'w_br_attn': nrm(ks[17], (DEPTH, D_V, D), D_V ** -0.5),
        'w_out': nrm(ks[18], (DEPTH, D, D), D ** -0.5),
        'peer_wq': nrm(ks[19], (DEPTH, D, PEER_HEADS * PEER_DQ), D ** -0.5),
        'peer_keys': nrm(ks[20], (DEPTH, PEER_HEADS, 2, N_KEYS, PEER_DQ // 2), (PEER_DQ // 2) ** -0.5),
        'peer_u': nrm(ks[21], (DEPTH, N_EXPERTS, D), D ** -0.5),
        'peer_v': nrm(ks[22], (DEPTH, N_EXPERTS, D), PEER_HEADS ** -0.5),
        'final_g': 1.0 + nrm(ks[23], (D,), 0.01),
    }


def reference(x, c, ctx, c_ctx, mod_w, mod_b, norm1_g, norm2_g, w_in, conv_w, conv_b,
              lru_w, lru_b, lru_lam, diff_lam, subln_g, w_br_lru, w_br_attn, w_out,
              peer_wq, peer_keys, peer_u, peer_v, final_g):
    L = x.shape[1]
    cos, sin = axial_rope_tables(L)
    s_lat = jax.nn.silu(c)
    s_ctx = jax.nn.silu(c_ctx)
    for li in range(DEPTH):
        update_ctx = li < DEPTH - 1
        lambda_init = 0.8 - 0.6 * math.exp(-0.3 * li)
        m_l = jnp.split((s_lat @ mod_w[li] + mod_b[li])[:, None, :], 6, axis=-1)
        m_c = jnp.split(s_ctx @ mod_w[li] + mod_b[li], 6, axis=-1)
        hx = modulate(rmsnorm(x, norm1_g[li]), m_l[0], m_l[1])
        hc = modulate(rmsnorm(ctx, norm1_g[li]), m_c[0], m_c[1])
        out_l, out_c = mixer(hx, hc, w_in[li], conv_w[li], conv_b[li], lru_w[li], lru_b[li],
                             lru_lam[li], diff_lam[li], subln_g[li], w_br_lru[li], w_br_attn[li],
                             w_out[li], lambda_init, cos, sin, update_ctx)
        x = x + m_l[2] * out_l
        x = x + m_l[5] * peer_ffn(modulate(rmsnorm(x, norm2_g[li]), m_l[3], m_l[4]),
                                  peer_wq[li], peer_keys[li], peer_u[li], peer_v[li])
        if update_ctx:
            ctx = ctx + m_c[2] * out_c
            ctx = ctx + m_c[5] * peer_ffn(modulate(rmsnorm(ctx, norm2_g[li]), m_c[3], m_c[4]),
                                          peer_wq[li], peer_keys[li], peer_u[li], peer_v[li])
    return rmsnorm(x, final_g)
```

```python
import functools
import math

import jax
import jax.numpy as jnp
from jax import lax
from jax.experimental import pallas as pl
from jax.experimental.pallas import tpu as pltpu

F32 = jnp.float32
BF16 = jnp.bfloat16

D_MODEL = 1024
GRID_W = 64
EPS = 1e-6
LRU_BLOCKS = 16
LRU_BW = D_MODEL // LRU_BLOCKS
CONV_W = 4
CONV_LEFT = 2
LRU_C = 8.0
ATTN_HEADS = 8
ATTN_DH = 64
ROPE_THETA = 10000.0
ROPE_FREQS = ATTN_DH // 4
PEER_HEADS = 8
N_KEYS = 128
PEER_TOPK = 16

LANES = 128
TOK_BLK = 256
MM_ROWS = 768
LRU_CHUNK = 256
PEER_TOK = 512
PEER_EC = 2048
NEG = -1e30
VMEM_LIMIT = 56 * 1024 * 1024


def _cparams(sem):
    return pltpu.CompilerParams(dimension_semantics=sem, vmem_limit_bytes=VMEM_LIMIT)


def _gelu(x):
    return 0.5 * x * (1.0 + jnp.tanh(0.7978845608028654 * (x + 0.044715 * (x * x * x))))


def _sigmoid(x):
    return 1.0 / (1.0 + jnp.exp(-x))


def _rms_mod(x, g, shift, scale):
    y = x * lax.rsqrt(jnp.mean(x * x, axis=-1, keepdims=True) + EPS)
    return (y * g) * (1.0 + scale) + shift


def _mods_kernel(c_ref, w_ref, b_ref, o_ref):
    c = c_ref[...]
    s = c * _sigmoid(c)
    o_ref[0] = jnp.dot(s, w_ref[0], preferred_element_type=F32,
                       precision=lax.Precision.HIGHEST) + b_ref[0]


def _mods(cc, mod_w, mod_b):
    depth, d, n = mod_w.shape
    tn = 1024
    return pl.pallas_call(
        _mods_kernel,
        out_shape=jax.ShapeDtypeStruct((depth, cc.shape[0], n), F32),
        grid=(depth, n // tn),
        in_specs=[
            pl.BlockSpec((cc.shape[0], d), lambda l, j: (0, 0)),
            pl.BlockSpec((1, d, tn), lambda l, j: (l, 0, j)),
            pl.BlockSpec((1, 1, tn), lambda l, j: (l, 0, j)),
        ],
        out_specs=pl.BlockSpec((1, cc.shape[0], tn), lambda l, j: (l, 0, j)),
        compiler_params=_cparams(("parallel", "parallel")),
        name="mods",
    )(cc, mod_w, mod_b.reshape(depth, 1, n))


def _norm_mod_kernel(x_ref, g_ref, m_ref, o_ref):
    o_ref[0] = _rms_mod(x_ref[0], g_ref[...], m_ref[0, 0, 0:1, :], m_ref[0, 0, 1:2, :]).astype(o_ref.dtype)


def _norm_mod(x, g, mods):
    b, t, d = x.shape
    return pl.pallas_call(
        _norm_mod_kernel,
        out_shape=jax.ShapeDtypeStruct((b, t, d), BF16),
        grid=(b, t // TOK_BLK),
        in_specs=[
            pl.BlockSpec((1, TOK_BLK, d), lambda bi, i: (bi, i, 0)),
            pl.BlockSpec((1, d), lambda bi, i: (0, 0)),
            pl.BlockSpec((1, 1, 6, d), lambda bi, i: (bi, jnp.minimum(i, 1), 0, 0)),
        ],
        out_specs=pl.BlockSpec((1, TOK_BLK, d), lambda bi, i: (bi, i, 0)),
        compiler_params=_cparams(("parallel", "parallel")),
        name="norm_mod",
    )(x, g.reshape(1, d), mods)


def _mm_kernel(a_ref, w_ref, o_ref):
    o_ref[0] = jnp.dot(a_ref[0], w_ref[...], preferred_element_type=F32).astype(o_ref.dtype)


def _mm_tm_kernel(a_ref, w_ref, o_ref):
    o_ref[...] = jnp.dot(a_ref[0], w_ref[...], preferred_element_type=F32).astype(o_ref.dtype)


def _mm_rows(t):
    return MM_ROWS if t % MM_ROWS == 0 else TOK_BLK


def _matmul(a, w, out_dtype):
    b, t, k = a.shape
    n = w.shape[1]
    tn = 1024
    tm = _mm_rows(t)
    return pl.pallas_call(
        _mm_kernel,
        out_shape=jax.ShapeDtypeStruct((b, t, n), out_dtype),
        grid=(n // tn, b, t // tm),
        in_specs=[
            pl.BlockSpec((1, tm, k), lambda j, bi, i: (bi, i, 0)),
            pl.BlockSpec((k, tn), lambda j, bi, i: (0, j)),
        ],
        out_specs=pl.BlockSpec((1, tm, tn), lambda j, bi, i: (bi, i, j)),
        compiler_params=_cparams(("parallel", "parallel", "parallel")),
        name="matmul",
    )(a, w)


def _matmul_time_major(a, w, out_dtype):
    b, t, k = a.shape
    n = w.shape[1]
    tm = _mm_rows(t)
    return pl.pallas_call(
        _mm_tm_kernel,
        out_shape=jax.ShapeDtypeStruct((t, b * n), out_dtype),
        grid=(b, t // tm),
        in_specs=[
            pl.BlockSpec((1, tm, k), lambda bi, i: (bi, i, 0)),
            pl.BlockSpec((k, n), lambda bi, i: (0, 0)),
        ],
        out_specs=pl.BlockSpec((tm, n), lambda bi, i: (i, bi)),
        compiler_params=_cparams(("parallel", "parallel")),
        name="matmul_time_major",
    )(a, w)


def _lru_kernel(u_ref, cw_ref, cb_ref, w_ref, b_ref, lam_ref, o_ref, a_s, d_s, *, ctx_len):
    t_total, nb, nc = u_ref.shape
    tc = LRU_CHUNK
    segments = ((0, ctx_len), (ctx_len, t_total))

    def conv_chunk(t0, seg):
        acc = jnp.broadcast_to(cb_ref[...].reshape(1, 1, nc), (tc, nb, nc))
        for k in range(CONV_W):
            lo = t0 + k - CONV_LEFT
            hi = lo + tc
            vlo, vhi = max(lo, seg[0]), min(hi, seg[1])
            piece = u_ref[vlo:vhi]
            if vlo > lo:
                piece = jnp.concatenate([jnp.zeros((vlo - lo, nb, nc), F32), piece], axis=0)
            if vhi < hi:
                piece = jnp.concatenate([piece, jnp.zeros((hi - vhi, nb, nc), F32)], axis=0)
            acc = acc + piece * cw_ref[k:k + 1, :].reshape(1, 1, nc)
        return acc

    def run_chunk(t0, seg, d, h):
        x = conv_chunk(t0, seg).reshape(tc * nb, nc)
        xb = x.astype(BF16)
        rec = _sigmoid(jnp.dot(xb, w_ref[d, 0, 0], preferred_element_type=F32) + b_ref[d, 0:1, :])
        inp = _sigmoid(jnp.dot(xb, w_ref[d, 1, 0], preferred_element_type=F32) + b_ref[d, 1:2, :])
        nlam = -lam_ref[d:d + 1, :]
        e = jnp.exp(-jnp.abs(nlam))
        ep1 = 1.0 + e
        sp = jnp.maximum(nlam, 0.0) + jnp.where(ep1 == 1.0, e, jnp.log(ep1) * (e / (ep1 - 1.0)))
        log_a = (-LRU_C) * rec * sp
        a = jnp.exp(log_a)
        drive = jnp.sqrt(1.0 - a * a) * (inp * x)
        a_s[...] = a.reshape(tc, nb, nc)
        d_s[...] = drive.reshape(tc, nb, nc)

        if d == 0:
            def step(i, hh):
                hh = a_s[i] * hh + d_s[i]
                o_ref[t0 + i] = hh
                return hh
        else:
            def step(i, hh):
                j = tc - 1 - i
                hh = a_s[j] * hh + d_s[j]
                o_ref[t0 + j] = o_ref[t0 + j] + hh
                return hh
        return lax.fori_loop(0, tc, step, h, unroll=8)

    chunks = [(t0, seg) for seg in segments for t0 in range(seg[0], seg[1], tc)]
    h = jnp.zeros((nb, nc), F32)
    for t0, seg in chunks:
        h = run_chunk(t0, seg, 0, h)
    h = jnp.zeros((nb, nc), F32)
    rev = [c for c in reversed(chunks) if c[1] == segments[0]] + [c for c in reversed(chunks) if c[1] == segments[1]]
    for t0, seg in rev:
        h = run_chunk(t0, seg, 1, h)


def _lru(u, conv_w, conv_b, wbd, lru_b, lru_lam, ctx_len):
    t, b, c = u.shape
    nc = LANES
    return pl.pallas_call(
        functools.partial(_lru_kernel, ctx_len=ctx_len),
        out_shape=jax.ShapeDtypeStruct((t, b, c), F32),
        grid=(c // nc,),
        in_specs=[
            pl.BlockSpec((t, b, nc), lambda j: (0, 0, j)),
            pl.BlockSpec((CONV_W, nc), lambda j: (0, j)),
            pl.BlockSpec((1, nc), lambda j: (0, j)),
            pl.BlockSpec((2, 2, 1, nc, nc), lambda j: (0, 0, j, 0, 0)),
            pl.BlockSpec((2, 2, nc), lambda j: (0, 0, j)),
            pl.BlockSpec((2, nc), lambda j: (0, j)),
        ],
        out_specs=pl.BlockSpec((t, b, nc), lambda j: (0, 0, j)),
        scratch_shapes=[pltpu.VMEM((LRU_CHUNK, b, nc), F32), pltpu.VMEM((LRU_CHUNK, b, nc), F32)],
        compiler_params=_cparams(("parallel",)),
        name="rglru",
    )(u, conv_w, conv_b.reshape(1, c), wbd, lru_b, lru_lam)


def _rope(x, c, sa, sb):
    return x * c + pltpu.roll(x, LANES - ROPE_FREQS, 1) * sa + pltpu.roll(x, ROPE_FREQS, 1) * sb


def _attn_kernel(q_ref, k_ref, v_ref, cq_ref, saq_ref, sbq_ref, ck_ref, sak_ref, sbk_ref, dl_ref, g_ref,
                 o_ref, kr_s, *, lambda_init, ctx_len, q_off):
    qi = pl.program_id(2)

    @pl.when(qi == 0)
    def _():
        k = k_ref[0].astype(F32)
        kr_s[...] = _rope(k, ck_ref[...], sak_ref[...], sbk_ref[...]).astype(BF16)

    dl = dl_ref[...]
    lam = (jnp.exp(jnp.sum(dl[0:1] * dl[1:2], axis=-1, keepdims=True))
           - jnp.exp(jnp.sum(dl[2:3] * dl[3:4], axis=-1, keepdims=True)) + lambda_init)
    q = _rope(q_ref[0].astype(F32), cq_ref[...], saq_ref[...], sbq_ref[...])
    lane = lax.broadcasted_iota(jnp.int32, q.shape, 1)
    q0 = jnp.where(lane < ATTN_DH, q, 0.0).astype(BF16)
    q1 = jnp.where(lane >= ATTN_DH, q, 0.0).astype(BF16)
    nt = (((1,), (1,)), ((), ()))

    def attend(nk):
        kr = kr_s[0:nk, :]

        def probs(qm):
            s = lax.dot_general(qm, kr, nt, preferred_element_type=F32)
            p = jnp.exp(s - jnp.max(s, axis=-1, keepdims=True))
            return p, jnp.sum(p, axis=-1, keepdims=True)

        p0, l0 = probs(q0)
        p1, l1 = probs(q1)
        w = p0 * (1.0 / l0) - p1 * (lam / l1)
        o = jnp.dot(w.astype(BF16), v_ref[0, 0:nk, :], preferred_element_type=F32)
        o = o * lax.rsqrt(jnp.mean(o * o, axis=-1, keepdims=True) + EPS)
        o_ref[0] = (o * (g_ref[...] * (1.0 - lambda_init))).astype(o_ref.dtype)

    nk_all = k_ref.shape[1]
    if q_off == 0:
        @pl.when(qi == 0)
        def _():
            attend(ctx_len)

        @pl.when(qi > 0)
        def _():
            attend(nk_all)
    else:
        attend(nk_all)


def _attention(p, tabs, diff_lam, subln_g, lambda_init, ctx_len, with_ctx):
    b, t, _ = p.shape
    tq = TOK_BLK
    q_off = 0 if with_ctx else ctx_len // tq
    nq = t // tq - q_off
    cq, saq, sbq, ck, sak, sbk = tabs
    hb = D_MODEL // LANES
    qspec = pl.BlockSpec((tq, LANES), lambda bi, h, i: (i + q_off, 0))
    kspec = pl.BlockSpec((t, LANES), lambda bi, h, i: (0, 0))
    return pl.pallas_call(
        functools.partial(_attn_kernel, lambda_init=lambda_init, ctx_len=ctx_len, q_off=q_off),
        out_shape=jax.ShapeDtypeStruct((b, nq * tq, D_MODEL), BF16),
        grid=(b, ATTN_HEADS, nq),
        in_specs=[
            pl.BlockSpec((1, tq, LANES), lambda bi, h, i: (bi, i + q_off, hb + h)),
            pl.BlockSpec((1, t, LANES), lambda bi, h, i: (bi, 0, 2 * hb + h)),
            pl.BlockSpec((1, t, LANES), lambda bi, h, i: (bi, 0, 3 * hb + h)),
            qspec, qspec, qspec, kspec, kspec, kspec,
            pl.BlockSpec((4, ATTN_DH), lambda bi, h, i: (0, 0)),
            pl.BlockSpec((1, LANES), lambda bi, h, i: (0, 0)),
        ],
        out_specs=pl.BlockSpec((1, tq, LANES), lambda bi, h, i: (bi, i, h)),
        scratch_shapes=[pltpu.VMEM((t, LANES), BF16)],
        compiler_params=_cparams(("parallel", "parallel", "arbitrary")),
        name="diff_attn",
    )(p, p, p, cq, saq, sbq, ck, sak, sbk, diff_lam, subln_g.reshape(1, LANES))


def _merge_kernel(hs_ref, gate_ref, ga_ref, gb_ref, ao_ref, x_ref, m_ref, g2_ref, wl_ref, wa_ref, wo_ref,
                  xo_ref, hp_ref):
    lru_in = (hs_ref[...] * _gelu(gate_ref[0].astype(F32))).astype(BF16)
    ya = jnp.dot(lru_in, wl_ref[...], preferred_element_type=F32)
    yb = jnp.dot(ao_ref[0], wa_ref[...], preferred_element_type=F32)
    mix = _sigmoid(ga_ref[0].astype(F32)) * ya + _sigmoid(gb_ref[0].astype(F32)) * yb
    out = jnp.dot(mix.astype(BF16), wo_ref[...], preferred_element_type=F32)
    xn = x_ref[0] + m_ref[0, 0, 2:3, :] * out
    xo_ref[0] = xn
    hp_ref[0] = _rms_mod(xn, g2_ref[...], m_ref[0, 0, 3:4, :], m_ref[0, 0, 4:5, :]).astype(hp_ref.dtype)


def _merge(hsum_tm, p, ao, x, mods, g2, wl, wa, wo, ctx_len, with_ctx):
    b, t, d = x.shape
    tm = TOK_BLK
    off = 0 if with_ctx else ctx_len // tm
    n = t // tm - off
    seg = (lambda i: jnp.minimum(i, 1)) if with_ctx else (lambda i: 1)
    wspec = pl.BlockSpec((d, d), lambda bi, i: (0, 0))
    return pl.pallas_call(
        _merge_kernel,
        out_shape=(jax.ShapeDtypeStruct((b, n * tm, d), F32), jax.ShapeDtypeStruct((b, n * tm, d), BF16)),
        grid=(b, n),
        in_specs=[
            pl.BlockSpec((tm, d), lambda bi, i: (i + off, bi)),
            pl.BlockSpec((1, tm, d), lambda bi, i: (bi, i + off, 0)),
            pl.BlockSpec((1, tm, d), lambda bi, i: (bi, i + off, 4)),
            pl.BlockSpec((1, tm, d), lambda bi, i: (bi, i + off, 5)),
            pl.BlockSpec((1, tm, d), lambda bi, i: (bi, i, 0)),
            pl.BlockSpec((1, tm, d), lambda bi, i: (bi, i + off, 0)),
            pl.BlockSpec((1, 1, 6, d), lambda bi, i: (bi, seg(i), 0, 0)),
            pl.BlockSpec((1, d), lambda bi, i: (0, 0)),
            wspec, wspec, wspec,
        ],
        out_specs=(pl.BlockSpec((1, tm, d), lambda bi, i: (bi, i, 0)),
                   pl.BlockSpec((1, tm, d), lambda bi, i: (bi, i, 0))),
        compiler_params=_cparams(("parallel", "parallel")),
        name="merge_out",
    )(hsum_tm, p, p, p, ao, x, mods, g2.reshape(1, d), wl, wa, wo)


def _top_values(s, n):
    vals = []
    for k in range(n):
        m = jnp.max(s, axis=0, keepdims=True)
        vals.append(m)
        if k + 1 < n:
            s = jnp.where(s == m, NEG, s)
    return vals


def _peer_kernel(hp_ref, wqt_ref, kkt_ref, u_ref, vt_ref, y_ref,
                 hpt_s, s1_s, s2_s, e1_s, e2_s, tau_s, v1_s, v2_s, g_s, acc_s):
    c = pl.program_id(1)
    tm = hp_ref.shape[0]
    kk = PEER_TOPK

    @pl.when(c == 0)
    def _():
        hpt = hp_ref[...].astype(F32).T.astype(BF16)
        hpt_s[...] = hpt
        qt = jnp.dot(wqt_ref[...], hpt, preferred_element_type=F32)
        row16 = lax.broadcasted_iota(jnp.int32, (kk, tm), 0)
        row8 = lax.broadcasted_iota(jnp.int32, (8, tm), 0)
        for h in range(PEER_HEADS):
            st = jnp.dot(kkt_ref[h], qt[h * LANES:(h + 1) * LANES].astype(BF16),
                         preferred_element_type=F32)
            s1 = st[0:N_KEYS]
            s2 = st[N_KEYS:2 * N_KEYS]
            v1 = _top_values(s1, kk)
            v2 = _top_values(s2, kk)
            for k in range(kk):
                v1_s[k:k + 1, :] = v1[k]
                v2_s[k:k + 1, :] = v2[k]
            v1a = v1_s[...]
            v2a = v2_s[...]
            tiles = [v1[0] + v2a]
            for a in range(1, 8):
                nb = kk // (a + 1)
                tiles.append(jnp.where(row8 < nb, v1[a] + v2a[0:8], NEG))
            tiles.append(v1a[8:16] + v2[0])
            cand = jnp.concatenate(tiles, axis=0)
            cv = _top_values(cand, kk)
            tau = cv[kk - 1]
            z = jnp.sum(jnp.where(cand >= tau, jnp.exp(cand - cv[0]), 0.0), axis=0, keepdims=True)
            s1_s[h] = s1
            s2_s[h] = s2
            e1_s[h] = jnp.exp(s1 - v1[0])
            e2_s[h] = jnp.exp(s2 - v2[0]) * (1.0 / z)
            tau_s[h:h + 1, :] = tau
        acc_s[...] = jnp.zeros_like(acc_s)

    hpt = hpt_s[...]
    n_slab = u_ref.shape[0] // N_KEYS

    def slab(j, carry):
        i1 = c * n_slab + j
        e0 = pl.multiple_of(j * N_KEYS, N_KEYS)
        a = jnp.dot(u_ref[pl.ds(e0, N_KEYS), :], hpt, preferred_element_type=F32)
        w = jnp.zeros((N_KEYS, tm), F32)
        for h in range(PEER_HEADS):
            s1r = s1_s[h, pl.ds(i1, 1), :]
            e1r = e1_s[h, pl.ds(i1, 1), :]
            w = w + jnp.where(s1r + s2_s[h] >= tau_s[h:h + 1, :], e1r * e2_s[h], 0.0)
        g_s[pl.ds(e0, N_KEYS), :] = (_gelu(a) * w).astype(BF16)
        return carry

    lax.fori_loop(0, n_slab, slab, 0)
    acc_s[...] += jnp.dot(vt_ref[...], g_s[...], preferred_element_type=F32)

    @pl.when(c == pl.num_programs(1) - 1)
    def _():
        y_ref[...] = acc_s[...].T


def _peer(hp, wqt, kkt, u, vt):
    n, d = hp.shape
    e = u.shape[0]
    tm, ec = PEER_TOK, PEER_EC
    return pl.pallas_call(
        _peer_kernel,
        out_shape=jax.ShapeDtypeStruct((n, d), F32),
        grid=(n // tm, e // ec),
        in_specs=[
            pl.BlockSpec((tm, d), lambda i, c: (i, 0)),
            pl.BlockSpec((d, d), lambda i, c: (0, 0)),
            pl.BlockSpec((PEER_HEADS, 2 * N_KEYS, LANES), lambda i, c: (0, 0, 0)),
            pl.BlockSpec((ec, d), lambda i, c: (c, 0)),
            pl.BlockSpec((d, ec), lambda i, c: (0, c)),
        ],
        out_specs=pl.BlockSpec((tm, d), lambda i, c: (i, 0)),
        scratch_shapes=[
            pltpu.VMEM((d, tm), BF16),
            pltpu.VMEM((PEER_HEADS, N_KEYS, tm), F32),
            pltpu.VMEM((PEER_HEADS, N_KEYS, tm), F32),
            pltpu.VMEM((PEER_HEADS, N_KEYS, tm), F32),
            pltpu.VMEM((PEER_HEADS, N_KEYS, tm), F32),
            pltpu.VMEM((PEER_HEADS, tm), F32),
            pltpu.VMEM((PEER_TOPK, tm), F32),
            pltpu.VMEM((PEER_TOPK, tm), F32),
            pltpu.VMEM((ec, tm), BF16),
            pltpu.VMEM((d, tm), F32),
        ],
        compiler_params=_cparams(("parallel", "arbitrary")),
        name="peer",
    )(hp, wqt, kkt, u, vt)


def _resid_norm_kernel(x_ref, y_ref, m_ref, mn_ref, g_ref, xo_ref, hn_ref):
    xn = x_ref[0] + m_ref[0, 0, 5:6, :] * y_ref[0]
    xo_ref[0] = xn
    hn_ref[0] = _rms_mod(xn, g_ref[...], mn_ref[0, 0, 0:1, :], mn_ref[0, 0, 1:2, :]).astype(hn_ref.dtype)


def _resid_final_kernel(x_ref, y_ref, m_ref, g_ref, o_ref):
    xn = x_ref[0] + m_ref[0, 0, 5:6, :] * y_ref[0]
    o_ref[0] = xn * lax.rsqrt(jnp.mean(xn * xn, axis=-1, keepdims=True) + EPS) * g_ref[...]


def _resid_norm(x, y, mods, mods_next, g_next):
    b, t, d = x.shape
    tm = TOK_BLK
    xspec = pl.BlockSpec((1, tm, d), lambda bi, i: (bi, i, 0))
    mspec = pl.BlockSpec((1, 1, 6, d), lambda bi, i: (bi, jnp.minimum(i, 1), 0, 0))
    return pl.pallas_call(
        _resid_norm_kernel,
        out_shape=(jax.ShapeDtypeStruct((b, t, d), F32), jax.ShapeDtypeStruct((b, t, d), BF16)),
        grid=(b, t // tm),
        in_specs=[xspec, xspec, mspec, mspec, pl.BlockSpec((1, d), lambda bi, i: (0, 0))],
        out_specs=(xspec, xspec),
        compiler_params=_cparams(("parallel", "parallel")),
        name="resid_norm",
    )(x, y, mods, mods_next, g_next.reshape(1, d))


def _resid_final(x, y, mods, g):
    b, t, d = x.shape
    tm = TOK_BLK
    xspec = pl.BlockSpec((1, tm, d), lambda bi, i: (bi, i, 0))
    return pl.pallas_call(
        _resid_final_kernel,
        out_shape=jax.ShapeDtypeStruct((b, t, d), F32),
        grid=(b, t // tm),
        in_specs=[xspec, xspec, pl.BlockSpec((1, 1, 6, d), lambda bi, i: (bi, 1, 0, 0)),
                  pl.BlockSpec((1, d), lambda bi, i: (0, 0))],
        out_specs=xspec,
        compiler_params=_cparams(("parallel", "parallel")),
        name="resid_final",
    )(x, y, mods, g.reshape(1, d))


def _rope_tables(n_latent, ctx_len):
    rows = n_latent // GRID_W
    row, col = jnp.meshgrid(jnp.arange(rows), jnp.arange(GRID_W), indexing='ij')
    pos = jnp.stack([row.reshape(-1), col.reshape(-1)], axis=-1).astype(F32)
    inv = ROPE_THETA ** (-jnp.arange(ROPE_FREQS, dtype=F32) / ROPE_FREQS)
    ang = pos[:, :, None] * inv
    cos, sin = jnp.cos(ang), jnp.sin(ang)
    lane = jnp.arange(LANES)
    axis = (lane % ATTN_DH) // (2 * ROPE_FREQS)
    freq = lane % ROPE_FREQS
    second = (lane % (2 * ROPE_FREQS)) >= ROPE_FREQS
    c = cos[:, axis, freq]
    s = sin[:, axis, freq]
    sa = jnp.where(second, 0.0, -s)
    sb = jnp.where(second, s, 0.0)
    ones = jnp.ones((ctx_len, LANES), F32)
    zeros = jnp.zeros((ctx_len, LANES), F32)
    ck = jnp.concatenate([ones, c], axis=0)
    sak = jnp.concatenate([zeros, sa], axis=0)
    sbk = jnp.concatenate([zeros, sb], axis=0)
    scale = ATTN_DH ** -0.5
    return ck * scale, sak * scale, sbk * scale, ck, sak, sbk


def _block_diag_pairs(w):
    lead = w.shape[:-3]
    w = w.reshape(lead + (LRU_BLOCKS // 2, 2, LRU_BW, LRU_BW))
    z = jnp.zeros_like(w[..., 0, :, :])
    top = jnp.concatenate([w[..., 0, :, :], z], axis=-1)
    bot = jnp.concatenate([z, w[..., 1, :, :]], axis=-1)
    return jnp.concatenate([top, bot], axis=-2)


def _peer_key_tiles(keys):
    z = jnp.zeros_like(keys[:, 0])
    top = jnp.concatenate([keys[:, 0], z], axis=-1)
    bot = jnp.concatenate([z, keys[:, 1]], axis=-1)
    return jnp.concatenate([top, bot], axis=1)


def kernel(x, c, ctx, c_ctx, mod_w, mod_b, norm1_g, norm2_g, w_in, conv_w, conv_b, lru_w, lru_b, lru_lam,
           diff_lam, subln_g, w_br_lru, w_br_attn, w_out, peer_wq, peer_keys, peer_u, peer_v, final_g):
    b, seq, d = x.shape
    ctx_len = ctx.shape[1]
    depth = mod_w.shape[0]
    t = ctx_len + seq

    cc = jnp.concatenate([c, c_ctx[None, :], jnp.zeros((16 - b - 1, d), F32)], axis=0)
    m = _mods(cc, mod_w, mod_b)
    m_lat = m[:, :b].reshape(depth, b, 1, 6, d)
    m_ctx = jnp.broadcast_to(m[:, b].reshape(depth, 1, 1, 6, d), (depth, b, 1, 6, d))
    mods = jnp.concatenate([m_ctx, m_lat], axis=2)

    tabs = _rope_tables(seq, ctx_len)
    xs = jnp.concatenate([ctx, x], axis=1)
    hn = _norm_mod(xs, norm1_g[0], mods[0])

    for li in range(depth):
        last = li == depth - 1
        lambda_init = 0.8 - 0.6 * math.exp(-0.3 * li)
        w_in_b = w_in[li].astype(BF16)
        u_tm = _matmul_time_major(hn, w_in_b[:, :d], F32)
        p = _matmul(hn, w_in_b[:, d:], BF16)
        hsum = _lru(u_tm.reshape(t, b, d), conv_w[li], conv_b[li],
                    _block_diag_pairs(lru_w[li]).astype(BF16), lru_b[li], lru_lam[li], ctx_len)
        ao = _attention(p, tabs, diff_lam[li], subln_g[li], lambda_init, ctx_len, not last)
        xm, hp = _merge(hsum.reshape(t, b * d), p, ao, xs, mods[li], norm2_g[li],
                        w_br_lru[li].astype(BF16), w_br_attn[li].astype(BF16), w_out[li].astype(BF16),
                        ctx_len, not last)
        nt = xm.shape[1]
        y = _peer(hp.reshape(b * nt, d), peer_wq[li].T.astype(BF16),
                  _peer_key_tiles(peer_keys[li]).astype(BF16),
                  peer_u[li].astype(BF16), peer_v[li].T.astype(BF16)).reshape(b, nt, d)
        if last:
            return _resid_final(xm, y, mods[li], final_g)
        xs, hn = _resid_norm(xm, y, mods[li], mods[li + 1], norm1_g[li + 1])
```

```python
import functools
import math

import jax
import jax.numpy as jnp
from jax import lax
from jax.experimental import pallas as pl
from jax.experimental.pallas import tpu as pltpu

F32 = jnp.float32
BF16 = jnp.bfloat16

D_MODEL = 1024
GRID_W = 64
EPS = 1e-6
LRU_BLOCKS = 16
LRU_BW = D_MODEL // LRU_BLOCKS
CONV_W = 4
CONV_LEFT = 2
LRU_C = 8.0
ATTN_HEADS = 8
ATTN_DH = 64
ROPE_THETA = 10000.0
ROPE_FREQS = ATTN_DH // 4
PEER_HEADS = 8
N_KEYS = 128
PEER_TOPK = 16

LANES = 128
TOK_BLK = 256
MM_ROWS = 768
LRU_CHUNK = 256
ATTN_HPS = 2
PEER_TOK = 512
PEER_EC = 2048
PEER_SUB = 512
PEER_PACK = 16
NEG = -1e30
VMEM_LIMIT = 56 * 1024 * 1024


def _cparams(sem):
    return pltpu.CompilerParams(dimension_semantics=sem, vmem_limit_bytes=VMEM_LIMIT)


def _gelu(x):
    return 0.5 * x * (1.0 + jnp.tanh(0.7978845608028654 * (x + 0.044715 * (x * x * x))))


def _sigmoid(x):
    return 1.0 / (1.0 + jnp.exp(-x))


def _rms_mod(x, g, shift, scale):
    y = x * lax.rsqrt(jnp.mean(x * x, axis=-1, keepdims=True) + EPS)
    return (y * g) * (1.0 + scale) + shift


def _mods_kernel(c_ref, w_ref, b_ref, o_ref):
    c = c_ref[...]
    s = c * _sigmoid(c)
    o_ref[0] = jnp.dot(s, w_ref[0], preferred_element_type=F32,
                       precision=lax.Precision.HIGHEST) + b_ref[0]


def _mods(cc, mod_w, mod_b):
    depth, d, n = mod_w.shape
    tn = 1024
    return pl.pallas_call(
        _mods_kernel,
        out_shape=jax.ShapeDtypeStruct((depth, cc.shape[0], n), F32),
        grid=(depth, n // tn),
        in_specs=[
            pl.BlockSpec((cc.shape[0], d), lambda l, j: (0, 0)),
            pl.BlockSpec((1, d, tn), lambda l, j: (l, 0, j)),
            pl.BlockSpec((1, 1, tn), lambda l, j: (l, 0, j)),
        ],
        out_specs=pl.BlockSpec((1, cc.shape[0], tn), lambda l, j: (l, 0, j)),
        compiler_params=_cparams(("parallel", "parallel")),
        name="mods",
    )(cc, mod_w, mod_b.reshape(depth, 1, n))


def _norm_mod_kernel(x_ref, g_ref, m_ref, o_ref):
    o_ref[0] = _rms_mod(x_ref[0], g_ref[...], m_ref[0, 0, 0:1, :], m_ref[0, 0, 1:2, :]).astype(o_ref.dtype)


def _norm_mod(x, g, mods):
    b, t, d = x.shape
    return pl.pallas_call(
        _norm_mod_kernel,
        out_shape=jax.ShapeDtypeStruct((b, t, d), BF16),
        grid=(b, t // TOK_BLK),
        in_specs=[
            pl.BlockSpec((1, TOK_BLK, d), lambda bi, i: (bi, i, 0)),
            pl.BlockSpec((1, d), lambda bi, i: (0, 0)),
            pl.BlockSpec((1, 1, 6, d), lambda bi, i: (bi, jnp.minimum(i, 1), 0, 0)),
        ],
        out_specs=pl.BlockSpec((1, TOK_BLK, d), lambda bi, i: (bi, i, 0)),
        compiler_params=_cparams(("parallel", "parallel")),
        name="norm_mod",
    )(x, g.reshape(1, d), mods)


def _mm_kernel(a_ref, w_ref, o_ref):
    o_ref[0] = jnp.dot(a_ref[0], w_ref[...], preferred_element_type=F32).astype(o_ref.dtype)


def _mm_tm_kernel(a_ref, w_ref, o_ref):
    o_ref[...] = jnp.dot(a_ref[0], w_ref[...], preferred_element_type=F32).astype(o_ref.dtype)


def _mm_rows(t):
    return MM_ROWS if t % MM_ROWS == 0 else TOK_BLK


def _matmul(a, w, out_dtype):
    b, t, k = a.shape
    n = w.shape[1]
    tn = 1024
    tm = _mm_rows(t)
    return pl.pallas_call(
        _mm_kernel,
        out_shape=jax.ShapeDtypeStruct((b, t, n), out_dtype),
        grid=(n // tn, b, t // tm),
        in_specs=[
            pl.BlockSpec((1, tm, k), lambda j, bi, i: (bi, i, 0)),
            pl.BlockSpec((k, tn), lambda j, bi, i: (0, j)),
        ],
        out_specs=pl.BlockSpec((1, tm, tn), lambda j, bi, i: (bi, i, j)),
        compiler_params=_cparams(("parallel", "parallel", "parallel")),
        name="matmul",
    )(a, w)


def _matmul_time_major(a, w, out_dtype):
    b, t, k = a.shape
    n = w.shape[1]
    tm = _mm_rows(t)
    return pl.pallas_call(
        _mm_tm_kernel,
        out_shape=jax.ShapeDtypeStruct((t, b * n), out_dtype),
        grid=(b, t // tm),
        in_specs=[
            pl.BlockSpec((1, tm, k), lambda bi, i: (bi, i, 0)),
            pl.BlockSpec((k, n), lambda bi, i: (0, 0)),
        ],
        out_specs=pl.BlockSpec((tm, n), lambda bi, i: (i, bi)),
        compiler_params=_cparams(("parallel", "parallel")),
        name="matmul_time_major",
    )(a, w)


def _lru_kernel(u_ref, cw_ref, cb_ref, w_ref, b_ref, lam_ref, o_ref, a_s, d_s, *, ctx_len):
    t_total, nb, nc = u_ref.shape
    tc = LRU_CHUNK
    segments = ((0, ctx_len), (ctx_len, t_total))

    def conv_chunk(t0, seg):
        acc = jnp.broadcast_to(cb_ref[...].reshape(1, 1, nc), (tc, nb, nc))
        for k in range(CONV_W):
            lo = t0 + k - CONV_LEFT
            hi = lo + tc
            vlo, vhi = max(lo, seg[0]), min(hi, seg[1])
            piece = u_ref[vlo:vhi]
            if vlo > lo:
                piece = jnp.concatenate([jnp.zeros((vlo - lo, nb, nc), F32), piece], axis=0)
            if vhi < hi:
                piece = jnp.concatenate([piece, jnp.zeros((hi - vhi, nb, nc), F32)], axis=0)
            acc = acc + piece * cw_ref[k:k + 1, :].reshape(1, 1, nc)
        return acc

    def run_chunk(t0, seg, d, h):
        x = conv_chunk(t0, seg).reshape(tc * nb, nc)
        xb = x.astype(BF16)
        rec = _sigmoid(jnp.dot(xb, w_ref[d, 0, 0], preferred_element_type=F32) + b_ref[d, 0:1, :])
        inp = _sigmoid(jnp.dot(xb, w_ref[d, 1, 0], preferred_element_type=F32) + b_ref[d, 1:2, :])
        nlam = -lam_ref[d:d + 1, :]
        e = jnp.exp(-jnp.abs(nlam))
        ep1 = 1.0 + e
        sp = jnp.maximum(nlam, 0.0) + jnp.where(ep1 == 1.0, e, jnp.log(ep1) * (e / (ep1 - 1.0)))
        log_a = (-LRU_C) * rec * sp
        a = jnp.exp(log_a)
        drive = jnp.sqrt(1.0 - a * a) * (inp * x)
        a_s[...] = a.reshape(tc, nb, nc)
        d_s[...] = drive.reshape(tc, nb, nc)

        if d == 0:
            def step(i, hh):
                hh = a_s[i] * hh + d_s[i]
                o_ref[t0 + i] = hh
                return hh
        else:
            def step(i, hh):
                j = tc - 1 - i
                hh = a_s[j] * hh + d_s[j]
                o_ref[t0 + j] = o_ref[t0 + j] + hh
                return hh
        return lax.fori_loop(0, tc, step, h, unroll=8)

    chunks = [(t0, seg) for seg in segments for t0 in range(seg[0], seg[1], tc)]
    h = jnp.zeros((nb, nc), F32)
    for t0, seg in chunks:
        h = run_chunk(t0, seg, 0, h)
    h = jnp.zeros((nb, nc), F32)
    rev = [c for c in reversed(chunks) if c[1] == segments[0]] + [c for c in reversed(chunks) if c[1] == segments[1]]
    for t0, seg in rev:
        h = run_chunk(t0, seg, 1, h)


def _lru(u, conv_w, conv_b, wbd, lru_b, lru_lam, ctx_len):
    t, b, c = u.shape
    nc = LANES
    return pl.pallas_call(
        functools.partial(_lru_kernel, ctx_len=ctx_len),
        out_shape=jax.ShapeDtypeStruct((t, b, c), F32),
        grid=(c // nc,),
        in_specs=[
            pl.BlockSpec((t, b, nc), lambda j: (0, 0, j)),
            pl.BlockSpec((CONV_W, nc), lambda j: (0, j)),
            pl.BlockSpec((1, nc), lambda j: (0, j)),
            pl.BlockSpec((2, 2, 1, nc, nc), lambda j: (0, 0, j, 0, 0)),
            pl.BlockSpec((2, 2, nc), lambda j: (0, 0, j)),
            pl.BlockSpec((2, nc), lambda j: (0, j)),
        ],
        out_specs=pl.BlockSpec((t, b, nc), lambda j: (0, 0, j)),
        scratch_shapes=[pltpu.VMEM((LRU_CHUNK, b, nc), F32), pltpu.VMEM((LRU_CHUNK, b, nc), F32)],
        compiler_params=_cparams(("parallel",)),
        name="rglru",
    )(u, conv_w, conv_b.reshape(1, c), wbd, lru_b, lru_lam)


def _rope(x, c, sa, sb):
    return x * c + pltpu.roll(x, LANES - ROPE_FREQS, 1) * sa + pltpu.roll(x, ROPE_FREQS, 1) * sb


def _attn_kernel(q_ref, k_ref, v_ref, cq_ref, saq_ref, sbq_ref, ck_ref, sak_ref, sbk_ref, dl_ref, g_ref,
                 o_ref, kr_s, *, lambda_init, ctx_len, q_off):
    qi = pl.program_id(2)
    heads = range(ATTN_HPS)

    @pl.when(qi == 0)
    def _():
        for hh in heads:
            k = k_ref[0, :, hh * LANES:(hh + 1) * LANES].astype(F32)
            kr_s[:, hh * LANES:(hh + 1) * LANES] = _rope(k, ck_ref[...], sak_ref[...], sbk_ref[...]).astype(BF16)

    dl = dl_ref[...]
    lam = (jnp.exp(jnp.sum(dl[0:1] * dl[1:2], axis=-1, keepdims=True))
           - jnp.exp(jnp.sum(dl[2:3] * dl[3:4], axis=-1, keepdims=True)) + lambda_init)
    nt = (((1,), (1,)), ((), ()))
    qm = []
    for hh in heads:
        q = _rope(q_ref[0, :, hh * LANES:(hh + 1) * LANES].astype(F32), cq_ref[...], saq_ref[...], sbq_ref[...])
        lane = lax.broadcasted_iota(jnp.int32, q.shape, 1)
        qm.append((jnp.where(lane < ATTN_DH, q, 0.0).astype(BF16), jnp.where(lane >= ATTN_DH, q, 0.0).astype(BF16)))

    def attend(nk):
        scores = []
        for hh in heads:
            kr = kr_s[0:nk, hh * LANES:(hh + 1) * LANES]
            scores.append([lax.dot_general(qx, kr, nt, preferred_element_type=F32) for qx in qm[hh]])
        for hh in heads:
            ps = []
            for s in scores[hh]:
                p = jnp.exp(s - jnp.max(s, axis=-1, keepdims=True))
                ps.append((p, jnp.sum(p, axis=-1, keepdims=True)))
            (p0, l0), (p1, l1) = ps
            w = p0 * (1.0 / l0) - p1 * (lam / l1)
            o = jnp.dot(w.astype(BF16), v_ref[0, 0:nk, hh * LANES:(hh + 1) * LANES], preferred_element_type=F32)
            o = o * lax.rsqrt(jnp.mean(o * o, axis=-1, keepdims=True) + EPS)
            o_ref[0, :, hh * LANES:(hh + 1) * LANES] = (o * (g_ref[...] * (1.0 - lambda_init))).astype(o_ref.dtype)

    nk_all = k_ref.shape[1]
    if q_off == 0:
        @pl.when(qi == 0)
        def _():
            attend(ctx_len)

        @pl.when(qi > 0)
        def _():
            attend(nk_all)
    else:
        attend(nk_all)


def _attention(p, tabs, diff_lam, subln_g, lambda_init, ctx_len, with_ctx):
    b, t, _ = p.shape
    tq = TOK_BLK
    q_off = 0 if with_ctx else ctx_len // tq
    nq = t // tq - q_off
    cq, saq, sbq, ck, sak, sbk = tabs
    wb = ATTN_HPS * LANES
    hb = D_MODEL // wb
    qspec = pl.BlockSpec((tq, LANES), lambda bi, h, i: (i + q_off, 0))
    kspec = pl.BlockSpec((t, LANES), lambda bi, h, i: (0, 0))
    return pl.pallas_call(
        functools.partial(_attn_kernel, lambda_init=lambda_init, ctx_len=ctx_len, q_off=q_off),
        out_shape=jax.ShapeDtypeStruct((b, nq * tq, D_MODEL), BF16),
        grid=(b, ATTN_HEADS // ATTN_HPS, nq),
        in_specs=[
            pl.BlockSpec((1, tq, wb), lambda bi, h, i: (bi, i + q_off, hb + h)),
            pl.BlockSpec((1, t, wb), lambda bi, h, i: (bi, 0, 2 * hb + h)),
            pl.BlockSpec((1, t, wb), lambda bi, h, i: (bi, 0, 3 * hb + h)),
            qspec, qspec, qspec, kspec, kspec, kspec,
            pl.BlockSpec((4, ATTN_DH), lambda bi, h, i: (0, 0)),
            pl.BlockSpec((1, LANES), lambda bi, h, i: (0, 0)),
        ],
        out_specs=pl.BlockSpec((1, tq, wb), lambda bi, h, i: (bi, i, h)),
        scratch_shapes=[pltpu.VMEM((t, wb), BF16)],
        compiler_params=_cparams(("parallel", "parallel", "arbitrary")),
        name="diff_attn",
    )(p, p, p, cq, saq, sbq, ck, sak, sbk, diff_lam, subln_g.reshape(1, LANES))


def _merge_kernel(hs_ref, gate_ref, ga_ref, gb_ref, ao_ref, x_ref, m_ref, g2_ref, wl_ref, wa_ref, wo_ref,
                  xo_ref, hp_ref):
    lru_in = (hs_ref[...] * _gelu(gate_ref[0].astype(F32))).astype(BF16)
    ya = jnp.dot(lru_in, wl_ref[...], preferred_element_type=F32)
    yb = jnp.dot(ao_ref[0], wa_ref[...], preferred_element_type=F32)
    mix = _sigmoid(ga_ref[0].astype(F32)) * ya + _sigmoid(gb_ref[0].astype(F32)) * yb
    out = jnp.dot(mix.astype(BF16), wo_ref[...], preferred_element_type=F32)
    xn = x_ref[0] + m_ref[0, 0, 2:3, :] * out
    xo_ref[0] = xn
    hp_ref[0] = _rms_mod(xn, g2_ref[...], m_ref[0, 0, 3:4, :], m_ref[0, 0, 4:5, :]).astype(hp_ref.dtype)


def _merge(hsum_tm, p, ao, x, mods, g2, wl, wa, wo, ctx_len, with_ctx):
    b, t, d = x.shape
    tm = TOK_BLK
    off = 0 if with_ctx else ctx_len // tm
    n = t // tm - off
    seg = (lambda i: jnp.minimum(i, 1)) if with_ctx else (lambda i: 1)
    wspec = pl.BlockSpec((d, d), lambda bi, i: (0, 0))
    return pl.pallas_call(
        _merge_kernel,
        out_shape=(jax.ShapeDtypeStruct((b, n * tm, d), F32), jax.ShapeDtypeStruct((b, n * tm, d), BF16)),
        grid=(b, n),
        in_specs=[
            pl.BlockSpec((tm, d), lambda bi, i: (i + off, bi)),
            pl.BlockSpec((1, tm, d), lambda bi, i: (bi, i + off, 0)),
            pl.BlockSpec((1, tm, d), lambda bi, i: (bi, i + off, 4)),
            pl.BlockSpec((1, tm, d), lambda bi, i: (bi, i + off, 5)),
            pl.BlockSpec((1, tm, d), lambda bi, i: (bi, i, 0)),
            pl.BlockSpec((1, tm, d), lambda bi, i: (bi, i + off, 0)),
            pl.BlockSpec((1, 1, 6, d), lambda bi, i: (bi, seg(i), 0, 0)),
            pl.BlockSpec((1, d), lambda bi, i: (0, 0)),
            wspec, wspec, wspec,
        ],
        out_specs=(pl.BlockSpec((1, tm, d), lambda bi, i: (bi, i, 0)),
                   pl.BlockSpec((1, tm, d), lambda bi, i: (bi, i, 0))),
        compiler_params=_cparams(("parallel", "parallel")),
        name="merge_out",
    )(hsum_tm, p, p, p, ao, x, mods, g2.reshape(1, d), wl, wa, wo)


def _top_ranked(s, n, want_rank=True):
    vals = []
    rank = jnp.full(s.shape, float(n), F32) if want_rank else None
    for k in range(n):
        m = jnp.max(s, axis=0, keepdims=True)
        vals.append(m)
        eq = s == m
        if want_rank:
            rank = jnp.where(eq, float(k), rank)
        if k + 1 < n:
            s = jnp.where(eq, NEG, s)
    return vals, rank


def _peer_kernel(hp_ref, wqt_ref, kkt_ref, u_ref, vt_ref, y_ref,
                 hpt_s, nb_s, e1_s, r2_s, e2_s, v1_s, v2_s, g0_s, g1_s, acc_s, *, n_chunk, n_steps):
    step = pl.program_id(0)
    c = step % n_chunk
    cy = (step + n_chunk - 1) % n_chunk
    tm = hp_ref.shape[0]
    kk = PEER_TOPK
    pk = PEER_PACK

    @pl.when(step == 0)
    def _():
        g1_s[...] = jnp.zeros_like(g1_s)
        acc_s[...] = jnp.zeros_like(acc_s)

    @pl.when((c == 0) & (step < n_steps - 1))
    def _():
        hpt = hp_ref[...].astype(F32).T.astype(BF16)
        hpt_s[...] = hpt
        qt = jnp.dot(wqt_ref[...], hpt, preferred_element_type=F32)
        row8 = lax.broadcasted_iota(jnp.int32, (8, tm), 0)
        for h in range(PEER_HEADS):
            st = jnp.dot(kkt_ref[h], qt[h * LANES:(h + 1) * LANES].astype(BF16),
                         preferred_element_type=F32)
            s1 = st[0:N_KEYS]
            s2 = st[N_KEYS:2 * N_KEYS]
            v1, r1 = _top_ranked(s1, kk)
            v2, r2 = _top_ranked(s2, kk)
            for k in range(kk):
                v1_s[k:k + 1, :] = v1[k]
                v2_s[k:k + 1, :] = v2[k]
            v1a = v1_s[...]
            v2a = v2_s[...]
            tiles = [v1[0] + v2a]
            for a in range(1, 8):
                nb = kk // (a + 1)
                tiles.append(jnp.where(row8 < nb, v1[a] + v2a[0:8], NEG))
            tiles.append(v1a[8:16] + v2[0])
            cand = jnp.concatenate(tiles, axis=0)
            cv, _ = _top_ranked(cand, kk, want_rank=False)
            tau = cv[kk - 1]
            z = jnp.sum(jnp.where(cand >= tau, jnp.exp(cand - cv[0]), 0.0), axis=0, keepdims=True)
            nbv = jnp.zeros((N_KEYS, tm), F32)
            for a in range(kk):
                cnt = jnp.sum(jnp.where(v1[a] + v2a >= tau, 1.0, 0.0), axis=0, keepdims=True)
                nbv = jnp.where(r1 == float(a), cnt, nbv)
            nb_s[h] = nbv
            e1_s[h] = jnp.exp(s1 - v1[0])
            r2_s[h] = r2.astype(BF16).reshape(N_KEYS // pk, pk, tm)
            e2_s[h] = (jnp.exp(s2 - v2[0]) * (1.0 / z)).astype(BF16).reshape(N_KEYS // pk, pk, tm)

    sub = PEER_SUB
    n_per = sub // N_KEYS
    n_sub = u_ref.shape[0] // sub
    zero = jnp.zeros((), BF16)
    keep = jnp.where(cy == 0, 0.0, 1.0).astype(F32)

    half = sub // 2
    n_here = half // N_KEYS

    def main(g_read, g_write):
        hpt = hpt_s[...]

        def a_piece(j, p):
            r0 = j * sub + p * half
            return jnp.dot(u_ref[r0:r0 + half, :], hpt, preferred_element_type=F32)

        def gate_piece(a, j, p):
            act = _gelu(a.astype(BF16)).reshape(n_here, N_KEYS // pk, pk, tm)
            for s in range(n_here):
                i1 = c * (u_ref.shape[0] // N_KEYS) + j * n_per + p * n_here + s
                w = None
                for h in range(PEER_HEADS):
                    nbb = jnp.broadcast_to(nb_s[h, pl.ds(i1, 1), :], (pk, tm)).astype(BF16)
                    e1b = jnp.broadcast_to(e1_s[h, pl.ds(i1, 1), :], (pk, tm)).astype(BF16)
                    term = jnp.where(r2_s[h] < nbb[None], e2_s[h], zero) * e1b[None]
                    w = term if w is None else w + term
                r0 = j * sub + p * half + s * N_KEYS
                g_write[r0:r0 + N_KEYS, :] = (act[s] * w).reshape(N_KEYS, tm)

        a_cur = [a_piece(0, 0), a_piece(0, 1)]
        y = None
        for j in range(n_sub):
            a_next = [None, None]
            for p in range(2):
                if j + 1 < n_sub:
                    a_next[p] = a_piece(j + 1, p)
                gate_piece(a_cur[p], j, p)
            yj = jnp.dot(vt_ref[:, j * sub:(j + 1) * sub], g_read[j * sub:(j + 1) * sub, :],
                         preferred_element_type=F32)
            y = yj if y is None else y + yj
            a_cur = a_next
        acc_s[...] = acc_s[...] * keep + y

    @pl.when(step % 2 == 0)
    def _():
        main(g1_s, g0_s)

    @pl.when(step % 2 == 1)
    def _():
        main(g0_s, g1_s)

    @pl.when((cy == n_chunk - 1) & (step > 0))
    def _():
        y_ref[...] = acc_s[...].T


def _peer(hp, wqt, kkt, u, vt):
    n, d = hp.shape
    e = u.shape[0]
    tm, ec = PEER_TOK, PEER_EC
    n_blk, n_chunk = n // tm, e // ec
    n_steps = n_blk * n_chunk + 1
    return pl.pallas_call(
        functools.partial(_peer_kernel, n_chunk=n_chunk, n_steps=n_steps),
        out_shape=jax.ShapeDtypeStruct((n, d), F32),
        grid=(n_steps,),
        in_specs=[
            pl.BlockSpec((tm, d), lambda s: (jnp.minimum(s // n_chunk, n_blk - 1), 0)),
            pl.BlockSpec((d, d), lambda s: (0, 0)),
            pl.BlockSpec((PEER_HEADS, 2 * N_KEYS, LANES), lambda s: (0, 0, 0)),
            pl.BlockSpec((ec, d), lambda s: (s % n_chunk, 0)),
            pl.BlockSpec((d, ec), lambda s: (0, (s + n_chunk - 1) % n_chunk)),
        ],
        out_specs=pl.BlockSpec((tm, d), lambda s: (jnp.maximum(s - 1, 0) // n_chunk, 0)),
        scratch_shapes=[
            pltpu.VMEM((d, tm), BF16),
            pltpu.VMEM((PEER_HEADS, N_KEYS, tm), F32),
            pltpu.VMEM((PEER_HEADS, N_KEYS, tm), F32),
            pltpu.VMEM((PEER_HEADS, N_KEYS // PEER_PACK, PEER_PACK, tm), BF16),
            pltpu.VMEM((PEER_HEADS, N_KEYS // PEER_PACK, PEER_PACK, tm), BF16),
            pltpu.VMEM((PEER_TOPK, tm), F32),
            pltpu.VMEM((PEER_TOPK, tm), F32),
            pltpu.VMEM((ec, tm), BF16),
            pltpu.VMEM((ec, tm), BF16),
            pltpu.VMEM((d, tm), F32),
        ],
        compiler_params=_cparams(("arbitrary",)),
        name="peer",
    )(hp, wqt, kkt, u, vt)


def _resid_norm_kernel(x_ref, y_ref, m_ref, mn_ref, g_ref, xo_ref, hn_ref):
    xn = x_ref[0] + m_ref[0, 0, 5:6, :] * y_ref[0]
    xo_ref[0] = xn
    hn_ref[0] = _rms_mod(xn, g_ref[...], mn_ref[0, 0, 0:1, :], mn_ref[0, 0, 1:2, :]).astype(hn_ref.dtype)


def _resid_final_kernel(x_ref, y_ref, m_ref, g_ref, o_ref):
    xn = x_ref[0] + m_ref[0, 0, 5:6, :] * y_ref[0]
    o_ref[0] = xn * lax.rsqrt(jnp.mean(xn * xn, axis=-1, keepdims=True) + EPS) * g_ref[...]


def _resid_norm(x, y, mods, mods_next, g_next):
    b, t, d = x.shape
    tm = TOK_BLK
    xspec = pl.BlockSpec((1, tm, d), lambda bi, i: (bi, i, 0))
    mspec = pl.BlockSpec((1, 1, 6, d), lambda bi, i: (bi, jnp.minimum(i, 1), 0, 0))
    return pl.pallas_call(
        _resid_norm_kernel,
        out_shape=(jax.ShapeDtypeStruct((b, t, d), F32), jax.ShapeDtypeStruct((b, t, d), BF16)),
        grid=(b, t // tm),
        in_specs=[xspec, xspec, mspec, mspec, pl.BlockSpec((1, d), lambda bi, i: (0, 0))],
        out_specs=(xspec, xspec),
        compiler_params=_cparams(("parallel", "parallel")),
        name="resid_norm",
    )(x, y, mods, mods_next, g_next.reshape(1, d))


def _resid_final(x, y, mods, g):
    b, t, d = x.shape
    tm = TOK_BLK
    xspec = pl.BlockSpec((1, tm, d), lambda bi, i: (bi, i, 0))
    return pl.pallas_call(
        _resid_final_kernel,
        out_shape=jax.ShapeDtypeStruct((b, t, d), F32),
        grid=(b, t // tm),
        in_specs=[xspec, xspec, pl.BlockSpec((1, 1, 6, d), lambda bi, i: (bi, 1, 0, 0)),
                  pl.BlockSpec((1, d), lambda bi, i: (0, 0))],
        out_specs=xspec,
        compiler_params=_cparams(("parallel", "parallel")),
        name="resid_final",
    )(x, y, mods, g.reshape(1, d))


def _rope_tables(n_latent, ctx_len):
    rows = n_latent // GRID_W
    row, col = jnp.meshgrid(jnp.arange(rows), jnp.arange(GRID_W), indexing='ij')
    pos = jnp.stack([row.reshape(-1), col.reshape(-1)], axis=-1).astype(F32)
    inv = ROPE_THETA ** (-jnp.arange(ROPE_FREQS, dtype=F32) / ROPE_FREQS)
    ang = pos[:, :, None] * inv
    cos, sin = jnp.cos(ang), jnp.sin(ang)
    lane = jnp.arange(LANES)
    axis = (lane % ATTN_DH) // (2 * ROPE_FREQS)
    freq = lane % ROPE_FREQS
    second = (lane % (2 * ROPE_FREQS)) >= ROPE_FREQS
    c = cos[:, axis, freq]
    s = sin[:, axis, freq]
    sa = jnp.where(second, 0.0, -s)
    sb = jnp.where(second, s, 0.0)
    ones = jnp.ones((ctx_len, LANES), F32)
    zeros = jnp.zeros((ctx_len, LANES), F32)
    ck = jnp.concatenate([ones, c], axis=0)
    sak = jnp.concatenate([zeros, sa], axis=0)
    sbk = jnp.concatenate([zeros, sb], axis=0)
    scale = ATTN_DH ** -0.5
    return ck * scale, sak * scale, sbk * scale, ck, sak, sbk


def _block_diag_pairs(w):
    lead = w.shape[:-3]
    w = w.reshape(lead + (LRU_BLOCKS // 2, 2, LRU_BW, LRU_BW))
    z = jnp.zeros_like(w[..., 0, :, :])
    top = jnp.concatenate([w[..., 0, :, :], z], axis=-1)
    bot = jnp.concatenate([z, w[..., 1, :, :]], axis=-1)
    return jnp.concatenate([top, bot], axis=-2)


def _peer_key_tiles(keys):
    z = jnp.zeros_like(keys[:, 0])
    top = jnp.concatenate([keys[:, 0], z], axis=-1)
    bot = jnp.concatenate([z, keys[:, 1]], axis=-1)
    return jnp.concatenate([top, bot], axis=1)


def kernel(x, c, ctx, c_ctx, mod_w, mod_b, norm1_g, norm2_g, w_in, conv_w, conv_b, lru_w, lru_b, lru_lam,
           diff_lam, subln_g, w_br_lru, w_br_attn, w_out, peer_wq, peer_keys, peer_u, peer_v, final_g):
    b, seq, d = x.shape
    ctx_len = ctx.shape[1]
    depth = mod_w.shape[0]
    t = ctx_len + seq

    cc = jnp.concatenate([c, c_ctx[None, :], jnp.zeros((16 - b - 1, d), F32)], axis=0)
    m = _mods(cc, mod_w, mod_b)
    m_lat = m[:, :b].reshape(depth, b, 1, 6, d)
    m_ctx = jnp.broadcast_to(m[:, b].reshape(depth, 1, 1, 6, d), (depth, b, 1, 6, d))
    mods = jnp.concatenate([m_ctx, m_lat], axis=2)

    tabs = _rope_tables(seq, ctx_len)
    xs = jnp.concatenate([ctx, x], axis=1)
    hn = _norm_mod(xs, norm1_g[0], mods[0])

    for li in range(depth):
        last = li == depth - 1
        lambda_init = 0.8 - 0.6 * math.exp(-0.3 * li)
        w_in_b = w_in[li].astype(BF16)
        u_tm = _matmul_time_major(hn, w_in_b[:, :d], F32)
        p = _matmul(hn, w_in_b[:, d:], BF16)
        hsum = _lru(u_tm.reshape(t, b, d), conv_w[li], conv_b[li],
                    _block_diag_pairs(lru_w[li]).astype(BF16), lru_b[li], lru_lam[li], ctx_len)
        ao = _attention(p, tabs, diff_lam[li], subln_g[li], lambda_init, ctx_len, not last)
        xm, hp = _merge(hsum.reshape(t, b * d), p, ao, xs, mods[li], norm2_g[li],
                        w_br_lru[li].astype(BF16), w_br_attn[li].astype(BF16), w_out[li].astype(BF16),
                        ctx_len, not last)
        nt = xm.shape[1]
        y = _peer(hp.reshape(b * nt, d), peer_wq[li].T.astype(BF16),
                  _peer_key_tiles(peer_keys[li]).astype(BF16),
                  peer_u[li].astype(BF16), peer_v[li].T.astype(BF16)).reshape(b, nt, d)
        if last:
            return _resid_final(xm, y, mods[li], final_g)
        xs, hn = _resid_norm(xm, y, mods[li], mods[li + 1], norm1_g[li + 1])
```

```python
import functools
import math

import jax
import jax.numpy as jnp
from jax import lax
from jax.experimental import pallas as pl
from jax.experimental.pallas import tpu as pltpu

F32 = jnp.float32
BF16 = jnp.bfloat16

D_MODEL = 1024
GRID_W = 64
EPS = 1e-6
LRU_BLOCKS = 16
LRU_BW = D_MODEL // LRU_BLOCKS
CONV_W = 4
CONV_LEFT = 2
LRU_C = 8.0
ATTN_HEADS = 8
ATTN_DH = 64
ROPE_THETA = 10000.0
ROPE_FREQS = ATTN_DH // 4
PEER_HEADS = 8
N_KEYS = 128
PEER_TOPK = 16

LANES = 128
TOK_BLK = 256
MM_ROWS = 768
LRU_CHUNK = 256
ATTN_HPS = 2
PEER_TOK = 512
PEER_EC = 2048
PEER_SUB = 512
PEER_PACK = 16
SC_PITCH = 260
TB_PITCH = 132
NEG = -1e30
VMEM_LIMIT = 56 * 1024 * 1024


def _cparams(sem):
    return pltpu.CompilerParams(dimension_semantics=sem, vmem_limit_bytes=VMEM_LIMIT)


def _gelu(x):
    return 0.5 * x * (1.0 + jnp.tanh(0.7978845608028654 * (x + 0.044715 * (x * x * x))))


def _sigmoid(x):
    return 1.0 / (1.0 + jnp.exp(-x))


def _rms_mod(x, g, shift, scale):
    y = x * lax.rsqrt(jnp.mean(x * x, axis=-1, keepdims=True) + EPS)
    return (y * g) * (1.0 + scale) + shift


def _mods_kernel(c_ref, w_ref, b_ref, o_ref):
    c = c_ref[...]
    s = c * _sigmoid(c)
    o_ref[0] = jnp.dot(s, w_ref[0], preferred_element_type=F32,
                       precision=lax.Precision.HIGHEST) + b_ref[0]


def _mods(cc, mod_w, mod_b):
    depth, d, n = mod_w.shape
    tn = 1024
    return pl.pallas_call(
        _mods_kernel,
        out_shape=jax.ShapeDtypeStruct((depth, cc.shape[0], n), F32),
        grid=(depth, n // tn),
        in_specs=[
            pl.BlockSpec((cc.shape[0], d), lambda l, j: (0, 0)),
            pl.BlockSpec((1, d, tn), lambda l, j: (l, 0, j)),
            pl.BlockSpec((1, 1, tn), lambda l, j: (l, 0, j)),
        ],
        out_specs=pl.BlockSpec((1, cc.shape[0], tn), lambda l, j: (l, 0, j)),
        compiler_params=_cparams(("parallel", "parallel")),
        name="mods",
    )(cc, mod_w, mod_b.reshape(depth, 1, n))


def _norm_mod_kernel(x_ref, g_ref, m_ref, o_ref):
    o_ref[0] = _rms_mod(x_ref[0], g_ref[...], m_ref[0, 0, 0:1, :], m_ref[0, 0, 1:2, :]).astype(o_ref.dtype)


def _norm_mod(x, g, mods):
    b, t, d = x.shape
    return pl.pallas_call(
        _norm_mod_kernel,
        out_shape=jax.ShapeDtypeStruct((b, t, d), BF16),
        grid=(b, t // TOK_BLK),
        in_specs=[
            pl.BlockSpec((1, TOK_BLK, d), lambda bi, i: (bi, i, 0)),
            pl.BlockSpec((1, d), lambda bi, i: (0, 0)),
            pl.BlockSpec((1, 1, 6, d), lambda bi, i: (bi, jnp.minimum(i, 1), 0, 0)),
        ],
        out_specs=pl.BlockSpec((1, TOK_BLK, d), lambda bi, i: (bi, i, 0)),
        compiler_params=_cparams(("parallel", "parallel")),
        name="norm_mod",
    )(x, g.reshape(1, d), mods)


def _mm_kernel(a_ref, w_ref, o_ref):
    o_ref[0] = jnp.dot(a_ref[0], w_ref[...], preferred_element_type=F32).astype(o_ref.dtype)


def _mm_tm_kernel(a_ref, w_ref, o_ref):
    o_ref[...] = jnp.dot(a_ref[0], w_ref[...], preferred_element_type=F32).astype(o_ref.dtype)


def _mm_rows(t):
    return MM_ROWS if t % MM_ROWS == 0 else TOK_BLK


def _matmul(a, w, out_dtype):
    b, t, k = a.shape
    n = w.shape[1]
    tn = 1024
    tm = _mm_rows(t)
    return pl.pallas_call(
        _mm_kernel,
        out_shape=jax.ShapeDtypeStruct((b, t, n), out_dtype),
        grid=(n // tn, b, t // tm),
        in_specs=[
            pl.BlockSpec((1, tm, k), lambda j, bi, i: (bi, i, 0)),
            pl.BlockSpec((k, tn), lambda j, bi, i: (0, j)),
        ],
        out_specs=pl.BlockSpec((1, tm, tn), lambda j, bi, i: (bi, i, j)),
        compiler_params=_cparams(("parallel", "parallel", "parallel")),
        name="matmul",
    )(a, w)


def _matmul_time_major(a, w, out_dtype):
    b, t, k = a.shape
    n = w.shape[1]
    tm = _mm_rows(t)
    return pl.pallas_call(
        _mm_tm_kernel,
        out_shape=jax.ShapeDtypeStruct((t, b * n), out_dtype),
        grid=(b, t // tm),
        in_specs=[
            pl.BlockSpec((1, tm, k), lambda bi, i: (bi, i, 0)),
            pl.BlockSpec((k, n), lambda bi, i: (0, 0)),
        ],
        out_specs=pl.BlockSpec((tm, n), lambda bi, i: (i, bi)),
        compiler_params=_cparams(("parallel", "parallel")),
        name="matmul_time_major",
    )(a, w)


def _lru_kernel(u_ref, cw_ref, cb_ref, w_ref, b_ref, lam_ref, o_ref, a_s, d_s, *, ctx_len):
    t_total, nb, nc = u_ref.shape
    tc = LRU_CHUNK
    segments = ((0, ctx_len), (ctx_len, t_total))

    def conv_chunk(t0, seg):
        acc = jnp.broadcast_to(cb_ref[...].reshape(1, 1, nc), (tc, nb, nc))
        for k in range(CONV_W):
            lo = t0 + k - CONV_LEFT
            hi = lo + tc
            vlo, vhi = max(lo, seg[0]), min(hi, seg[1])
            piece = u_ref[vlo:vhi]
            if vlo > lo:
                piece = jnp.concatenate([jnp.zeros((vlo - lo, nb, nc), F32), piece], axis=0)
            if vhi < hi:
                piece = jnp.concatenate([piece, jnp.zeros((hi - vhi, nb, nc), F32)], axis=0)
            acc = acc + piece * cw_ref[k:k + 1, :].reshape(1, 1, nc)
        return acc

    def run_chunk(t0, seg, d, h):
        x = conv_chunk(t0, seg).reshape(tc * nb, nc)
        xb = x.astype(BF16)
        rec = _sigmoid(jnp.dot(xb, w_ref[d, 0, 0], preferred_element_type=F32) + b_ref[d, 0:1, :])
        inp = _sigmoid(jnp.dot(xb, w_ref[d, 1, 0], preferred_element_type=F32) + b_ref[d, 1:2, :])
        nlam = -lam_ref[d:d + 1, :]
        e = jnp.exp(-jnp.abs(nlam))
        ep1 = 1.0 + e
        sp = jnp.maximum(nlam, 0.0) + jnp.where(ep1 == 1.0, e, jnp.log(ep1) * (e / (ep1 - 1.0)))
        log_a = (-LRU_C) * rec * sp
        a = jnp.exp(log_a)
        drive = jnp.sqrt(1.0 - a * a) * (inp * x)
        a_s[...] = a.reshape(tc, nb, nc)
        d_s[...] = drive.reshape(tc, nb, nc)

        if d == 0:
            def step(i, hh):
                hh = a_s[i] * hh + d_s[i]
                o_ref[t0 + i] = hh
                return hh
        else:
            def step(i, hh):
                j = tc - 1 - i
                hh = a_s[j] * hh + d_s[j]
                o_ref[t0 + j] = o_ref[t0 + j] + hh
                return hh
        return lax.fori_loop(0, tc, step, h, unroll=8)

    chunks = [(t0, seg) for seg in segments for t0 in range(seg[0], seg[1], tc)]
    h = jnp.zeros((nb, nc), F32)
    for t0, seg in chunks:
        h = run_chunk(t0, seg, 0, h)
    h = jnp.zeros((nb, nc), F32)
    rev = [c for c in reversed(chunks) if c[1] == segments[0]] + [c for c in reversed(chunks) if c[1] == segments[1]]
    for t0, seg in rev:
        h = run_chunk(t0, seg, 1, h)


def _lru(u, conv_w, conv_b, wbd, lru_b, lru_lam, ctx_len):
    t, b, c = u.shape
    nc = LANES
    return pl.pallas_call(
        functools.partial(_lru_kernel, ctx_len=ctx_len),
        out_shape=jax.ShapeDtypeStruct((t, b, c), F32),
        grid=(c // nc,),
        in_specs=[
            pl.BlockSpec((t, b, nc), lambda j: (0, 0, j)),
            pl.BlockSpec((CONV_W, nc), lambda j: (0, j)),
            pl.BlockSpec((1, nc), lambda j: (0, j)),
            pl.BlockSpec((2, 2, 1, nc, nc), lambda j: (0, 0, j, 0, 0)),
            pl.BlockSpec((2, 2, nc), lambda j: (0, 0, j)),
            pl.BlockSpec((2, nc), lambda j: (0, j)),
        ],
        out_specs=pl.BlockSpec((t, b, nc), lambda j: (0, 0, j)),
        scratch_shapes=[pltpu.VMEM((LRU_CHUNK, b, nc), F32), pltpu.VMEM((LRU_CHUNK, b, nc), F32)],
        compiler_params=_cparams(("parallel",)),
        name="rglru",
    )(u, conv_w, conv_b.reshape(1, c), wbd, lru_b, lru_lam)


def _rope(x, c, sa, sb):
    return x * c + pltpu.roll(x, LANES - ROPE_FREQS, 1) * sa + pltpu.roll(x, ROPE_FREQS, 1) * sb


def _attn_kernel(q_ref, k_ref, v_ref, cq_ref, saq_ref, sbq_ref, ck_ref, sak_ref, sbk_ref, dl_ref, g_ref,
                 o_ref, kr_s, *, lambda_init, ctx_len, q_off):
    qi = pl.program_id(2)
    heads = range(ATTN_HPS)

    @pl.when(qi == 0)
    def _():
        for hh in heads:
            k = k_ref[0, :, hh * LANES:(hh + 1) * LANES].astype(F32)
            kr_s[:, hh * LANES:(hh + 1) * LANES] = _rope(k, ck_ref[...], sak_ref[...], sbk_ref[...]).astype(BF16)

    dl = dl_ref[...]
    lam = (jnp.exp(jnp.sum(dl[0:1] * dl[1:2], axis=-1, keepdims=True))
           - jnp.exp(jnp.sum(dl[2:3] * dl[3:4], axis=-1, keepdims=True)) + lambda_init)
    nt = (((1,), (1,)), ((), ()))
    qm = []
    for hh in heads:
        q = _rope(q_ref[0, :, hh * LANES:(hh + 1) * LANES].astype(F32), cq_ref[...], saq_ref[...], sbq_ref[...])
        lane = lax.broadcasted_iota(jnp.int32, q.shape, 1)
        qm.append((jnp.where(lane < ATTN_DH, q, 0.0).astype(BF16), jnp.where(lane >= ATTN_DH, q, 0.0).astype(BF16)))

    def attend(nk):
        scores = []
        for hh in heads:
            kr = kr_s[0:nk, hh * LANES:(hh + 1) * LANES]
            scores.append([lax.dot_general(qx, kr, nt, preferred_element_type=F32) for qx in qm[hh]])
        for hh in heads:
            ps = []
            for s in scores[hh]:
                p = jnp.exp(s - jnp.max(s, axis=-1, keepdims=True))
                ps.append((p, jnp.sum(p, axis=-1, keepdims=True)))
            (p0, l0), (p1, l1) = ps
            w = p0 * (1.0 / l0) - p1 * (lam / l1)
            o = jnp.dot(w.astype(BF16), v_ref[0, 0:nk, hh * LANES:(hh + 1) * LANES], preferred_element_type=F32)
            o = o * lax.rsqrt(jnp.mean(o * o, axis=-1, keepdims=True) + EPS)
            o_ref[0, :, hh * LANES:(hh + 1) * LANES] = (o * (g_ref[...] * (1.0 - lambda_init))).astype(o_ref.dtype)

    nk_all = k_ref.shape[1]
    if q_off == 0:
        @pl.when(qi == 0)
        def _():
            attend(ctx_len)

        @pl.when(qi > 0)
        def _():
            attend(nk_all)
    else:
        attend(nk_all)


def _attention(p, tabs, diff_lam, subln_g, lambda_init, ctx_len, with_ctx):
    b, t, _ = p.shape
    tq = TOK_BLK
    q_off = 0 if with_ctx else ctx_len // tq
    nq = t // tq - q_off
    cq, saq, sbq, ck, sak, sbk = tabs
    wb = ATTN_HPS * LANES
    hb = D_MODEL // wb
    qspec = pl.BlockSpec((tq, LANES), lambda bi, h, i: (i + q_off, 0))
    kspec = pl.BlockSpec((t, LANES), lambda bi, h, i: (0, 0))
    return pl.pallas_call(
        functools.partial(_attn_kernel, lambda_init=lambda_init, ctx_len=ctx_len, q_off=q_off),
        out_shape=jax.ShapeDtypeStruct((b, nq * tq, D_MODEL), BF16),
        grid=(b, ATTN_HEADS // ATTN_HPS, nq),
        in_specs=[
            pl.BlockSpec((1, tq, wb), lambda bi, h, i: (bi, i + q_off, hb + h)),
            pl.BlockSpec((1, t, wb), lambda bi, h, i: (bi, 0, 2 * hb + h)),
            pl.BlockSpec((1, t, wb), lambda bi, h, i: (bi, 0, 3 * hb + h)),
            qspec, qspec, qspec, kspec, kspec, kspec,
            pl.BlockSpec((4, ATTN_DH), lambda bi, h, i: (0, 0)),
            pl.BlockSpec((1, LANES), lambda bi, h, i: (0, 0)),
        ],
        out_specs=pl.BlockSpec((1, tq, wb), lambda bi, h, i: (bi, i, h)),
        scratch_shapes=[pltpu.VMEM((t, wb), BF16)],
        compiler_params=_cparams(("parallel", "parallel", "arbitrary")),
        name="diff_attn",
    )(p, p, p, cq, saq, sbq, ck, sak, sbk, diff_lam, subln_g.reshape(1, LANES))


def _merge_kernel(hs_ref, gate_ref, ga_ref, gb_ref, ao_ref, x_ref, m_ref, g2_ref, wl_ref, wa_ref, wo_ref,
                  xo_ref, hp_ref):
    lru_in = (hs_ref[...] * _gelu(gate_ref[0].astype(F32))).astype(BF16)
    ya = jnp.dot(lru_in, wl_ref[...], preferred_element_type=F32)
    yb = jnp.dot(ao_ref[0], wa_ref[...], preferred_element_type=F32)
    mix = _sigmoid(ga_ref[0].astype(F32)) * ya + _sigmoid(gb_ref[0].astype(F32)) * yb
    out = jnp.dot(mix.astype(BF16), wo_ref[...], preferred_element_type=F32)
    xn = x_ref[0] + m_ref[0, 0, 2:3, :] * out
    xo_ref[0] = xn
    hp_ref[0] = _rms_mod(xn, g2_ref[...], m_ref[0, 0, 3:4, :], m_ref[0, 0, 4:5, :]).astype(hp_ref.dtype)


def _merge(hsum_tm, p, ao, x, mods, g2, wl, wa, wo, ctx_len, with_ctx):
    b, t, d = x.shape
    tm = TOK_BLK
    off = 0 if with_ctx else ctx_len // tm
    n = t // tm - off
    seg = (lambda i: jnp.minimum(i, 1)) if with_ctx else (lambda i: 1)
    wspec = pl.BlockSpec((d, d), lambda bi, i: (0, 0))
    return pl.pallas_call(
        _merge_kernel,
        out_shape=(jax.ShapeDtypeStruct((b, n * tm, d), F32), jax.ShapeDtypeStruct((b, n * tm, d), BF16)),
        grid=(b, n),
        in_specs=[
            pl.BlockSpec((tm, d), lambda bi, i: (i + off, bi)),
            pl.BlockSpec((1, tm, d), lambda bi, i: (bi, i + off, 0)),
            pl.BlockSpec((1, tm, d), lambda bi, i: (bi, i + off, 4)),
            pl.BlockSpec((1, tm, d), lambda bi, i: (bi, i + off, 5)),
            pl.BlockSpec((1, tm, d), lambda bi, i: (bi, i, 0)),
            pl.BlockSpec((1, tm, d), lambda bi, i: (bi, i + off, 0)),
            pl.BlockSpec((1, 1, 6, d), lambda bi, i: (bi, seg(i), 0, 0)),
            pl.BlockSpec((1, d), lambda bi, i: (0, 0)),
            wspec, wspec, wspec,
        ],
        out_specs=(pl.BlockSpec((1, tm, d), lambda bi, i: (bi, i, 0)),
                   pl.BlockSpec((1, tm, d), lambda bi, i: (bi, i, 0))),
        compiler_params=_cparams(("parallel", "parallel")),
        name="merge_out",
    )(hsum_tm, p, p, p, ao, x, mods, g2.reshape(1, d), wl, wa, wo)


def _sort_network(n):
    pairs = []
    p = 1
    while p < n:
        k = p
        while k >= 1:
            j = k % p
            while j <= n - 1 - k:
                for i in range(min(k - 1, n - j - k - 1) + 1):
                    if (i + j) // (2 * p) == (i + j + k) // (2 * p):
                        pairs.append((i + j, i + j + k))
                j += 2 * k
            k //= 2
        p *= 2
    return pairs


def _sorted_desc(vals):
    vals = list(vals)
    for i, j in _sort_network(len(vals)):
        vals[i], vals[j] = jnp.maximum(vals[i], vals[j]), jnp.minimum(vals[i], vals[j])
    return vals


def _merge_top(a, b):
    n = len(a)
    z = [jnp.maximum(a[j], b[n - 1 - j]) for j in range(n)]
    d = n // 2
    while d >= 1:
        for j in range(n):
            if not j & d:
                z[j], z[j + d] = jnp.maximum(z[j], z[j + d]), jnp.minimum(z[j], z[j + d])
        d //= 2
    return z


def _count_leading(test, pivots):
    v = pivots
    b8 = test(v[7])
    b4 = test(jnp.where(b8, v[11], v[3]))
    b2 = test(jnp.where(b8, jnp.where(b4, v[13], v[9]), jnp.where(b4, v[5], v[1])))
    b1 = test(jnp.where(b8, jnp.where(b4, jnp.where(b2, v[14], v[12]), jnp.where(b2, v[10], v[8])),
                        jnp.where(b4, jnp.where(b2, v[6], v[4]), jnp.where(b2, v[2], v[0]))))
    lo = (jnp.where(b8, 8.0, 0.0) + jnp.where(b4, 4.0, 0.0)) + (jnp.where(b2, 2.0, 0.0) + jnp.where(b1, 1.0, 0.0))
    return jnp.where(test(v[15]), 16.0, lo)


def _peer_kernel(hp_ref, wqt_ref, kkt_ref, u_ref, vt_ref, y_ref,
                 hpt_s, qt_s, st_s, nbf_s, e1f_s, r2f_s, e2f_s, nb_s, e1_s, r2_s, e2_s, g0_s, g1_s, acc_s,
                 *, n_chunk, n_steps):
    step = pl.program_id(0)
    c = step % n_chunk
    cy = (step + n_chunk - 1) % n_chunk
    tm = hp_ref.shape[0]
    kk = PEER_TOPK
    pk = PEER_PACK

    @pl.when(step == 0)
    def _():
        g1_s[...] = jnp.zeros_like(g1_s)
        acc_s[...] = jnp.zeros_like(acc_s)

    @pl.when((c == 0) & (step < n_steps - 1))
    def _():
        hpt = hp_ref[...].astype(F32).T.astype(BF16)
        hpt_s[...] = hpt
        qt_s[...] = jnp.dot(wqt_ref[...], hpt, preferred_element_type=F32)
        n_tt = tm // LANES
        per = 8 // n_tt

        def head_group(grp, carry):
            for m in range(8):
                h = grp * per + m // n_tt
                tt = m % n_tt
                qh = qt_s[pl.ds(pl.multiple_of(h * LANES, LANES), LANES), tt * LANES:(tt + 1) * LANES]
                st_s[m * SC_PITCH:m * SC_PITCH + 2 * N_KEYS, :] = jnp.dot(
                    kkt_ref[h], qh.astype(BF16), preferred_element_type=F32)

            def key(k):
                return st_s[pl.ds(k, 8, stride=SC_PITCH), :]

            def top(first):
                t = _sorted_desc([key(first + k) for k in range(kk)])
                for g in range(1, N_KEYS // kk):
                    t = _merge_top(t, _sorted_desc([key(first + g * kk + k) for k in range(kk)]))
                return t

            v1 = top(0)
            v2 = top(N_KEYS)
            rows = [[v1[a] + v2[b] for b in range(kk // (a + 1))] for a in range(kk)]
            neg = jnp.full((8, LANES), NEG, F32)
            best = rows[0]
            best = _merge_top(best, rows[1] + [neg] * (kk - len(rows[1])))
            mid = [x for r in rows[2:7] for x in r]
            best = _merge_top(best, _sorted_desc(mid + [neg] * (kk - len(mid))))
            tail = [x for r in rows[7:] for x in r]
            best = _merge_top(best, _sorted_desc(tail + [neg] * (kk - len(tail))))
            tau = best[kk - 1]
            z = None
            for x in best:
                e = jnp.exp(x - best[0])
                z = e if z is None else z + e
            rz = 1.0 / z
            for k in range(N_KEYS):
                s1 = key(k)
                nb = _count_leading(lambda p: s1 + p >= tau, v2)
                nbf_s[pl.ds(k, 8, stride=TB_PITCH), :] = jnp.where(s1 >= v1[kk - 1], nb, 0.0)
                e1f_s[pl.ds(k, 8, stride=TB_PITCH), :] = jnp.exp(s1 - v1[0])
                s2 = key(N_KEYS + k)
                r2f_s[pl.ds(k, 8, stride=TB_PITCH), :] = _count_leading(lambda p: p > s2, v2)
                e2f_s[pl.ds(k, 8, stride=TB_PITCH), :] = jnp.exp(s2 - v2[0]) * rz
            for m in range(8):
                h = grp * per + m // n_tt
                lanes = slice((m % n_tt) * LANES, (m % n_tt + 1) * LANES)
                rws = slice(m * TB_PITCH, m * TB_PITCH + N_KEYS)
                nb_s[h, :, lanes] = nbf_s[rws, :]
                e1_s[h, :, lanes] = e1f_s[rws, :]
                r2_s[h, :, :, lanes] = r2f_s[rws, :].astype(BF16).reshape(N_KEYS // pk, pk, LANES)
                e2_s[h, :, :, lanes] = e2f_s[rws, :].astype(BF16).reshape(N_KEYS // pk, pk, LANES)
            return carry

        lax.fori_loop(0, PEER_HEADS // per, head_group, 0)

    sub = PEER_SUB
    n_per = sub // N_KEYS
    n_sub = u_ref.shape[0] // sub
    zero = jnp.zeros((), BF16)
    keep = jnp.where(cy == 0, 0.0, 1.0).astype(F32)

    half = sub // 2
    n_here = half // N_KEYS

    def main(g_read, g_write):
        hpt = hpt_s[...]

        def a_piece(j, p):
            r0 = j * sub + p * half
            return jnp.dot(u_ref[r0:r0 + half, :], hpt, preferred_element_type=F32)

        def gate_piece(a, j, p, after):
            zrow = 0.0
            if after is not None:
                bits = pltpu.bitcast(after[after.shape[0] - 8:, :], jnp.uint32)
                bits = lax.shift_right_logical(lax.shift_right_logical(bits, jnp.uint32(16)), jnp.uint32(16))
                zrow = pltpu.bitcast(bits, F32)[0:1, :]
            act = _gelu(a.astype(BF16)).reshape(n_here, N_KEYS // pk, pk, tm)
            for s in range(n_here):
                i1 = c * (u_ref.shape[0] // N_KEYS) + j * n_per + p * n_here + s
                w = None
                for h in range(PEER_HEADS):
                    nbb = jnp.broadcast_to(nb_s[h, pl.ds(i1, 1), :] + zrow, (pk, tm)).astype(BF16)
                    e1b = jnp.broadcast_to(e1_s[h, pl.ds(i1, 1), :], (pk, tm)).astype(BF16)
                    term = jnp.where(r2_s[h] < nbb[None], e2_s[h], zero) * e1b[None]
                    w = term if w is None else w + term
                r0 = j * sub + p * half + s * N_KEYS
                g_write[r0:r0 + N_KEYS, :] = (act[s] * w).reshape(N_KEYS, tm)

        def y_slice(j):
            return jnp.dot(vt_ref[:, j * sub:(j + 1) * sub], g_read[j * sub:(j + 1) * sub, :],
                           preferred_element_type=F32)

        a_cur = [a_piece(0, 0), a_piece(0, 1)]
        y = None
        for j in range(n_sub):
            a_next = [None, None]
            for p in range(2):
                if j + 1 < n_sub:
                    a_next[p] = a_piece(j + 1, p)
                gate_piece(a_cur[p], j, p, a_next[p])
            yj = y_slice(j)
            y = yj if y is None else y + yj
            a_cur = a_next
        acc_s[...] = acc_s[...] * keep + y

    @pl.when(step % 2 == 0)
    def _():
        main(g1_s, g0_s)

    @pl.when(step % 2 == 1)
    def _():
        main(g0_s, g1_s)

    @pl.when((cy == n_chunk - 1) & (step > 0))
    def _():
        y_ref[...] = acc_s[...].T


def _peer(hp, wqt, kkt, u, vt):
    n, d = hp.shape
    e = u.shape[0]
    tm, ec = PEER_TOK, PEER_EC
    n_blk, n_chunk = n // tm, e // ec
    n_steps = n_blk * n_chunk + 1
    return pl.pallas_call(
        functools.partial(_peer_kernel, n_chunk=n_chunk, n_steps=n_steps),
        out_shape=jax.ShapeDtypeStruct((n, d), F32),
        grid=(n_steps,),
        in_specs=[
            pl.BlockSpec((tm, d), lambda s: (jnp.minimum(s // n_chunk, n_blk - 1), 0)),
            pl.BlockSpec((d, d), lambda s: (0, 0)),
            pl.BlockSpec((PEER_HEADS, 2 * N_KEYS, LANES), lambda s: (0, 0, 0)),
            pl.BlockSpec((ec, d), lambda s: (s % n_chunk, 0)),
            pl.BlockSpec((d, ec), lambda s: (0, (s + n_chunk - 1) % n_chunk)),
        ],
        out_specs=pl.BlockSpec((tm, d), lambda s: (jnp.maximum(s - 1, 0) // n_chunk, 0)),
        scratch_shapes=[
            pltpu.VMEM((d, tm), BF16),
            pltpu.VMEM((d, tm), F32),
            pltpu.VMEM((8 * SC_PITCH, LANES), F32),
            pltpu.VMEM((8 * TB_PITCH, LANES), F32),
            pltpu.VMEM((8 * TB_PITCH, LANES), F32),
            pltpu.VMEM((8 * TB_PITCH, LANES), F32),
            pltpu.VMEM((8 * TB_PITCH, LANES), F32),
            pltpu.VMEM((PEER_HEADS, N_KEYS, tm), F32),
            pltpu.VMEM((PEER_HEADS, N_KEYS, tm), F32),
            pltpu.VMEM((PEER_HEADS, N_KEYS // PEER_PACK, PEER_PACK, tm), BF16),
            pltpu.VMEM((PEER_HEADS, N_KEYS // PEER_PACK, PEER_PACK, tm), BF16),
            pltpu.VMEM((ec, tm), BF16),
            pltpu.VMEM((ec, tm), BF16),
            pltpu.VMEM((d, tm), F32),
        ],
        compiler_params=_cparams(("arbitrary",)),
        name="peer",
    )(hp, wqt, kkt, u, vt)


def _resid_norm_kernel(x_ref, y_ref, m_ref, mn_ref, g_ref, xo_ref, hn_ref):
    xn = x_ref[0] + m_ref[0, 0, 5:6, :] * y_ref[0]
    xo_ref[0] = xn
    hn_ref[0] = _rms_mod(xn, g_ref[...], mn_ref[0, 0, 0:1, :], mn_ref[0, 0, 1:2, :]).astype(hn_ref.dtype)


def _resid_final_kernel(x_ref, y_ref, m_ref, g_ref, o_ref):
    xn = x_ref[0] + m_ref[0, 0, 5:6, :] * y_ref[0]
    o_ref[0] = xn * lax.rsqrt(jnp.mean(xn * xn, axis=-1, keepdims=True) + EPS) * g_ref[...]


def _resid_norm(x, y, mods, mods_next, g_next):
    b, t, d = x.shape
    tm = TOK_BLK
    xspec = pl.BlockSpec((1, tm, d), lambda bi, i: (bi, i, 0))
    mspec = pl.BlockSpec((1, 1, 6, d), lambda bi, i: (bi, jnp.minimum(i, 1), 0, 0))
    return pl.pallas_call(
        _resid_norm_kernel,
        out_shape=(jax.ShapeDtypeStruct((b, t, d), F32), jax.ShapeDtypeStruct((b, t, d), BF16)),
        grid=(b, t // tm),
        in_specs=[xspec, xspec, mspec, mspec, pl.BlockSpec((1, d), lambda bi, i: (0, 0))],
        out_specs=(xspec, xspec),
        compiler_params=_cparams(("parallel", "parallel")),
        name="resid_norm",
    )(x, y, mods, mods_next, g_next.reshape(1, d))


def _resid_final(x, y, mods, g):
    b, t, d = x.shape
    tm = TOK_BLK
    xspec = pl.BlockSpec((1, tm, d), lambda bi, i: (bi, i, 0))
    return pl.pallas_call(
        _resid_final_kernel,
        out_shape=jax.ShapeDtypeStruct((b, t, d), F32),
        grid=(b, t // tm),
        in_specs=[xspec, xspec, pl.BlockSpec((1, 1, 6, d), lambda bi, i: (bi, 1, 0, 0)),
                  pl.BlockSpec((1, d), lambda bi, i: (0, 0))],
        out_specs=xspec,
        compiler_params=_cparams(("parallel", "parallel")),
        name="resid_final",
    )(x, y, mods, g.reshape(1, d))


def _rope_tables(n_latent, ctx_len):
    rows = n_latent // GRID_W
    row, col = jnp.meshgrid(jnp.arange(rows), jnp.arange(GRID_W), indexing='ij')
    pos = jnp.stack([row.reshape(-1), col.reshape(-1)], axis=-1).astype(F32)
    inv = ROPE_THETA ** (-jnp.arange(ROPE_FREQS, dtype=F32) / ROPE_FREQS)
    ang = pos[:, :, None] * inv
    cos, sin = jnp.cos(ang), jnp.sin(ang)
    lane = jnp.arange(LANES)
    axis = (lane % ATTN_DH) // (2 * ROPE_FREQS)
    freq = lane % ROPE_FREQS
    second = (lane % (2 * ROPE_FREQS)) >= ROPE_FREQS
    c = cos[:, axis, freq]
    s = sin[:, axis, freq]
    sa = jnp.where(second, 0.0, -s)
    sb = jnp.where(second, s, 0.0)
    ones = jnp.ones((ctx_len, LANES), F32)
    zeros = jnp.zeros((ctx_len, LANES), F32)
    ck = jnp.concatenate([ones, c], axis=0)
    sak = jnp.concatenate([zeros, sa], axis=0)
    sbk = jnp.concatenate([zeros, sb], axis=0)
    scale = ATTN_DH ** -0.5
    return ck * scale, sak * scale, sbk * scale, ck, sak, sbk


def _block_diag_pairs(w):
    lead = w.shape[:-3]
    w = w.reshape(lead + (LRU_BLOCKS // 2, 2, LRU_BW, LRU_BW))
    z = jnp.zeros_like(w[..., 0, :, :])
    top = jnp.concatenate([w[..., 0, :, :], z], axis=-1)
    bot = jnp.concatenate([z, w[..., 1, :, :]], axis=-1)
    return jnp.concatenate([top, bot], axis=-2)


def _peer_key_tiles(keys):
    z = jnp.zeros_like(keys[:, 0])
    top = jnp.concatenate([keys[:, 0], z], axis=-1)
    bot = jnp.concatenate([z, keys[:, 1]], axis=-1)
    return jnp.concatenate([top, bot], axis=1)


def kernel(x, c, ctx, c_ctx, mod_w, mod_b, norm1_g, norm2_g, w_in, conv_w, conv_b, lru_w, lru_b, lru_lam,
           diff_lam, subln_g, w_br_lru, w_br_attn, w_out, peer_wq, peer_keys, peer_u, peer_v, final_g):
    b, seq, d = x.shape
    ctx_len = ctx.shape[1]
    depth = mod_w.shape[0]
    t = ctx_len + seq

    cc = jnp.concatenate([c, c_ctx[None, :], jnp.zeros((16 - b - 1, d), F32)], axis=0)
    m = _mods(cc, mod_w, mod_b)
    m_lat = m[:, :b].reshape(depth, b, 1, 6, d)
    m_ctx = jnp.broadcast_to(m[:, b].reshape(depth, 1, 1, 6, d), (depth, b, 1, 6, d))
    mods = jnp.concatenate([m_ctx, m_lat], axis=2)

    tabs = _rope_tables(seq, ctx_len)
    xs = jnp.concatenate([ctx, x], axis=1)
    hn = _norm_mod(xs, norm1_g[0], mods[0])

    for li in range(depth):
        last = li == depth - 1
        lambda_init = 0.8 - 0.6 * math.exp(-0.3 * li)
        w_in_b = w_in[li].astype(BF16)
        u_tm = _matmul_time_major(hn, w_in_b[:, :d], F32)
        p = _matmul(hn, w_in_b[:, d:], BF16)
        hsum = _lru(u_tm.reshape(t, b, d), conv_w[li], conv_b[li],
                    _block_diag_pairs(lru_w[li]).astype(BF16), lru_b[li], lru_lam[li], ctx_len)
        ao = _attention(p, tabs, diff_lam[li], subln_g[li], lambda_init, ctx_len, not last)
        xm, hp = _merge(hsum.reshape(t, b * d), p, ao, xs, mods[li], norm2_g[li],
                        w_br_lru[li].astype(BF16), w_br_attn[li].astype(BF16), w_out[li].astype(BF16),
                        ctx_len, not last)
        nt = xm.shape[1]
        y = _peer(hp.reshape(b * nt, d), peer_wq[li].T.astype(BF16),
                  _peer_key_tiles(peer_keys[li]).astype(BF16),
                  peer_u[li].astype(BF16), peer_v[li].T.astype(BF16)).reshape(b, nt, d)
        if last:
            return _resid_final(xm, y, mods[li], final_g)
        xs, hn = _resid_norm(xm, y, mods[li], mods[li + 1], norm1_g[li + 1])
```

```python
import functools
import math

import jax
import jax.numpy as jnp
from jax import lax
from jax.experimental import pallas as pl
from jax.experimental.pallas import tpu as pltpu

F32 = jnp.float32
BF16 = jnp.bfloat16

D_MODEL = 1024
GRID_W = 64
EPS = 1e-6
LRU_BLOCKS = 16
LRU_BW = D_MODEL // LRU_BLOCKS
CONV_W = 4
CONV_LEFT = 2
LRU_C = 8.0
ATTN_HEADS = 8
ATTN_DH = 64
ROPE_THETA = 10000.0
ROPE_FREQS = ATTN_DH // 4
PEER_HEADS = 8
N_KEYS = 128
PEER_TOPK = 16

LANES = 128
TOK_BLK = 256
MM_ROWS = 768
LRU_CHUNK = 256
ATTN_HPS = 4
PEER_TOK = 512
PEER_EC = 2048
PEER_SUB = 512
PEER_PACK = 16
SC_PITCH = 260
TB_PITCH = 132
NEG = -1e30
VMEM_LIMIT = 56 * 1024 * 1024


def _cparams(sem):
    return pltpu.CompilerParams(dimension_semantics=sem, vmem_limit_bytes=VMEM_LIMIT)


def _gelu(x):
    return 0.5 * x * (1.0 + jnp.tanh(0.7978845608028654 * (x + 0.044715 * (x * x * x))))


def _sigmoid(x):
    return 1.0 / (1.0 + jnp.exp(-x))


def _rms_mod(x, g, shift, scale):
    y = x * lax.rsqrt(jnp.mean(x * x, axis=-1, keepdims=True) + EPS)
    return (y * g) * (1.0 + scale) + shift


def _mods_kernel(c_ref, w_ref, b_ref, o_ref):
    c = c_ref[...]
    s = c * _sigmoid(c)
    o_ref[0] = jnp.dot(s, w_ref[0], preferred_element_type=F32,
                       precision=lax.Precision.HIGHEST) + b_ref[0]


def _mods(cc, mod_w, mod_b):
    depth, d, n = mod_w.shape
    tn = 1024
    return pl.pallas_call(
        _mods_kernel,
        out_shape=jax.ShapeDtypeStruct((depth, cc.shape[0], n), F32),
        grid=(depth, n // tn),
        in_specs=[
            pl.BlockSpec((cc.shape[0], d), lambda l, j: (0, 0)),
            pl.BlockSpec((1, d, tn), lambda l, j: (l, 0, j)),
            pl.BlockSpec((1, 1, tn), lambda l, j: (l, 0, j)),
        ],
        out_specs=pl.BlockSpec((1, cc.shape[0], tn), lambda l, j: (l, 0, j)),
        compiler_params=_cparams(("parallel", "parallel")),
        name="mods",
    )(cc, mod_w, mod_b.reshape(depth, 1, n))


def _norm_mod_kernel(x_ref, g_ref, m_ref, o_ref):
    o_ref[0] = _rms_mod(x_ref[0], g_ref[...], m_ref[0, 0, 0:1, :], m_ref[0, 0, 1:2, :]).astype(o_ref.dtype)


def _norm_mod(x, g, mods):
    b, t, d = x.shape
    return pl.pallas_call(
        _norm_mod_kernel,
        out_shape=jax.ShapeDtypeStruct((b, t, d), BF16),
        grid=(b, t // TOK_BLK),
        in_specs=[
            pl.BlockSpec((1, TOK_BLK, d), lambda bi, i: (bi, i, 0)),
            pl.BlockSpec((1, d), lambda bi, i: (0, 0)),
            pl.BlockSpec((1, 1, 6, d), lambda bi, i: (bi, jnp.minimum(i, 1), 0, 0)),
        ],
        out_specs=pl.BlockSpec((1, TOK_BLK, d), lambda bi, i: (bi, i, 0)),
        compiler_params=_cparams(("parallel", "parallel")),
        name="norm_mod",
    )(x, g.reshape(1, d), mods)


def _mm_kernel(a_ref, w_ref, o_ref):
    o_ref[0] = jnp.dot(a_ref[0], w_ref[...], preferred_element_type=F32).astype(o_ref.dtype)


def _mm_tm_kernel(a_ref, w_ref, o_ref):
    o_ref[...] = jnp.dot(a_ref[0], w_ref[...], preferred_element_type=F32).astype(o_ref.dtype)


def _mm_rows(t):
    return MM_ROWS if t % MM_ROWS == 0 else TOK_BLK


def _matmul(a, w, out_dtype):
    b, t, k = a.shape
    n = w.shape[1]
    tn = 1024
    tm = _mm_rows(t)
    return pl.pallas_call(
        _mm_kernel,
        out_shape=jax.ShapeDtypeStruct((b, t, n), out_dtype),
        grid=(n // tn, b, t // tm),
        in_specs=[
            pl.BlockSpec((1, tm, k), lambda j, bi, i: (bi, i, 0)),
            pl.BlockSpec((k, tn), lambda j, bi, i: (0, j)),
        ],
        out_specs=pl.BlockSpec((1, tm, tn), lambda j, bi, i: (bi, i, j)),
        compiler_params=_cparams(("parallel", "parallel", "parallel")),
        name="matmul",
    )(a, w)


def _matmul_time_major(a, w, out_dtype):
    b, t, k = a.shape
    n = w.shape[1]
    tm = _mm_rows(t)
    return pl.pallas_call(
        _mm_tm_kernel,
        out_shape=jax.ShapeDtypeStruct((t, b * n), out_dtype),
        grid=(b, t // tm),
        in_specs=[
            pl.BlockSpec((1, tm, k), lambda bi, i: (bi, i, 0)),
            pl.BlockSpec((k, n), lambda bi, i: (0, 0)),
        ],
        out_specs=pl.BlockSpec((tm, n), lambda bi, i: (i, bi)),
        compiler_params=_cparams(("parallel", "parallel")),
        name="matmul_time_major",
    )(a, w)


def _lru_kernel(u_ref, cw_ref, cb_ref, w_ref, b_ref, lam_ref, o_ref, a_s, d_s, hb_s, *, ctx_len):
    t_total, nb, nc = u_ref.shape
    tc = LRU_CHUNK
    segments = ((0, ctx_len), (ctx_len, t_total))

    def conv_chunk(t0, seg):
        acc = jnp.broadcast_to(cb_ref[...].reshape(1, 1, nc), (tc, nb, nc))
        for k in range(CONV_W):
            lo = t0 + k - CONV_LEFT
            hi = lo + tc
            vlo, vhi = max(lo, seg[0]), min(hi, seg[1])
            piece = u_ref[vlo:vhi]
            if vlo > lo:
                piece = jnp.concatenate([jnp.zeros((vlo - lo, nb, nc), F32), piece], axis=0)
            if vhi < hi:
                piece = jnp.concatenate([piece, jnp.zeros((hi - vhi, nb, nc), F32)], axis=0)
            acc = acc + piece * cw_ref[k:k + 1, :].reshape(1, 1, nc)
        return acc

    def run_chunk(t0, seg, d, h):
        x = conv_chunk(t0, seg).reshape(tc * nb, nc)
        xb = x.astype(BF16)
        rec = _sigmoid(jnp.dot(xb, w_ref[d, 0, 0], preferred_element_type=F32) + b_ref[d, 0:1, :])
        inp = _sigmoid(jnp.dot(xb, w_ref[d, 1, 0], preferred_element_type=F32) + b_ref[d, 1:2, :])
        nlam = -lam_ref[d:d + 1, :]
        e = jnp.exp(-jnp.abs(nlam))
        ep1 = 1.0 + e
        sp = jnp.maximum(nlam, 0.0) + jnp.where(ep1 == 1.0, e, jnp.log(ep1) * (e / (ep1 - 1.0)))
        a = jnp.exp2(rec * (sp * (-LRU_C * math.log2(math.e))))
        drive = jnp.sqrt(1.0 - a * a) * (inp * x)
        a_s[...] = a.reshape(tc, nb, nc)
        d_s[...] = drive.reshape(tc, nb, nc)

        if d == 0:
            def step(i, hh):
                hh = a_s[i] * hh + d_s[i]
                o_ref[t0 + i] = hh
                return hh
            return lax.fori_loop(0, tc, step, h, unroll=8)

        def step(i, hh):
            j = tc - 1 - i
            hh = a_s[j] * hh + d_s[j]
            hb_s[j] = hh
            return hh
        h = lax.fori_loop(0, tc, step, h, unroll=8)
        o_ref[t0:t0 + tc] = o_ref[t0:t0 + tc] + hb_s[...]
        return h

    chunks = [(t0, seg) for seg in segments for t0 in range(seg[0], seg[1], tc)]
    h = jnp.zeros((nb, nc), F32)
    for t0, seg in chunks:
        h = run_chunk(t0, seg, 0, h)
    h = jnp.zeros((nb, nc), F32)
    rev = [c for c in reversed(chunks) if c[1] == segments[0]] + [c for c in reversed(chunks) if c[1] == segments[1]]
    for t0, seg in rev:
        h = run_chunk(t0, seg, 1, h)


def _lru(u, conv_w, conv_b, wbd, lru_b, lru_lam, ctx_len):
    t, b, c = u.shape
    nc = LANES
    return pl.pallas_call(
        functools.partial(_lru_kernel, ctx_len=ctx_len),
        out_shape=jax.ShapeDtypeStruct((t, b, c), F32),
        grid=(c // nc,),
        in_specs=[
            pl.BlockSpec((t, b, nc), lambda j: (0, 0, j)),
            pl.BlockSpec((CONV_W, nc), lambda j: (0, j)),
            pl.BlockSpec((1, nc), lambda j: (0, j)),
            pl.BlockSpec((2, 2, 1, nc, nc), lambda j: (0, 0, j, 0, 0)),
            pl.BlockSpec((2, 2, nc), lambda j: (0, 0, j)),
            pl.BlockSpec((2, nc), lambda j: (0, j)),
        ],
        out_specs=pl.BlockSpec((t, b, nc), lambda j: (0, 0, j)),
        scratch_shapes=[pltpu.VMEM((LRU_CHUNK, b, nc), F32)] * 3,
        compiler_params=_cparams(("parallel",)),
        name="rglru",
    )(u, conv_w, conv_b.reshape(1, c), wbd, lru_b, lru_lam)


def _rope(x, c, sa, sb):
    return x * c + pltpu.roll(x, LANES - ROPE_FREQS, 1) * sa + pltpu.roll(x, ROPE_FREQS, 1) * sb


def _attn_kernel(q_ref, k_ref, v_ref, cq_ref, saq_ref, sbq_ref, ck_ref, sak_ref, sbk_ref, dl_ref, g_ref,
                 o_ref, kr_s, *, lambda_init, ctx_len, q_off):
    qi = pl.program_id(2)
    heads = range(ATTN_HPS)

    @pl.when(qi == 0)
    def _():
        for hh in heads:
            k = k_ref[0, :, hh * LANES:(hh + 1) * LANES].astype(F32)
            kr_s[:, hh * LANES:(hh + 1) * LANES] = _rope(k, ck_ref[...], sak_ref[...], sbk_ref[...]).astype(BF16)

    dl = dl_ref[...]
    lam = (jnp.exp(jnp.sum(dl[0:1] * dl[1:2], axis=-1, keepdims=True))
           - jnp.exp(jnp.sum(dl[2:3] * dl[3:4], axis=-1, keepdims=True)) + lambda_init)
    nt = (((1,), (1,)), ((), ()))
    qm = []
    for hh in heads:
        q = _rope(q_ref[0, :, hh * LANES:(hh + 1) * LANES].astype(F32), cq_ref[...], saq_ref[...], sbq_ref[...])
        lane = lax.broadcasted_iota(jnp.int32, q.shape, 1)
        qm.append((jnp.where(lane < ATTN_DH, q, 0.0).astype(BF16), jnp.where(lane >= ATTN_DH, q, 0.0).astype(BF16)))

    def attend(nk):
        scores = []
        for hh in heads:
            kr = kr_s[0:nk, hh * LANES:(hh + 1) * LANES]
            scores.append([lax.dot_general(qx, kr, nt, preferred_element_type=F32) for qx in qm[hh]])
        for hh in heads:
            ps = []
            for s in scores[hh]:
                p = jnp.exp2(s - jnp.max(s, axis=-1, keepdims=True))
                ps.append((p, jnp.sum(p, axis=-1, keepdims=True)))
            (p0, l0), (p1, l1) = ps
            w = p0 * (1.0 / l0) - p1 * (lam / l1)
            o = jnp.dot(w.astype(BF16), v_ref[0, 0:nk, hh * LANES:(hh + 1) * LANES], preferred_element_type=F32)
            o = o * lax.rsqrt(jnp.mean(o * o, axis=-1, keepdims=True) + EPS)
            o_ref[0, :, hh * LANES:(hh + 1) * LANES] = (o * (g_ref[...] * (1.0 - lambda_init))).astype(o_ref.dtype)

    nk_all = k_ref.shape[1]
    if q_off == 0:
        @pl.when(qi == 0)
        def _():
            attend(ctx_len)

        @pl.when(qi > 0)
        def _():
            attend(nk_all)
    else:
        attend(nk_all)


def _attention(p, tabs, diff_lam, subln_g, lambda_init, ctx_len, with_ctx):
    b, t, _ = p.shape
    tq = TOK_BLK
    q_off = 0 if with_ctx else ctx_len // tq
    nq = t // tq - q_off
    cq, saq, sbq, ck, sak, sbk = tabs
    wb = ATTN_HPS * LANES
    hb = D_MODEL // wb
    qspec = pl.BlockSpec((tq, LANES), lambda bi, h, i: (i + q_off, 0))
    kspec = pl.BlockSpec((t, LANES), lambda bi, h, i: (0, 0))
    return pl.pallas_call(
        functools.partial(_attn_kernel, lambda_init=lambda_init, ctx_len=ctx_len, q_off=q_off),
        out_shape=jax.ShapeDtypeStruct((b, nq * tq, D_MODEL), BF16),
        grid=(b, ATTN_HEADS // ATTN_HPS, nq),
        in_specs=[
            pl.BlockSpec((1, tq, wb), lambda bi, h, i: (bi, i + q_off, hb + h)),
            pl.BlockSpec((1, t, wb), lambda bi, h, i: (bi, 0, 2 * hb + h)),
            pl.BlockSpec((1, t, wb), lambda bi, h, i: (bi, 0, 3 * hb + h)),
            qspec, qspec, qspec, kspec, kspec, kspec,
            pl.BlockSpec((4, ATTN_DH), lambda bi, h, i: (0, 0)),
            pl.BlockSpec((1, LANES), lambda bi, h, i: (0, 0)),
        ],
        out_specs=pl.BlockSpec((1, tq, wb), lambda bi, h, i: (bi, i, h)),
        scratch_shapes=[pltpu.VMEM((t, wb), BF16)],
        compiler_params=_cparams(("parallel", "parallel", "arbitrary")),
        name="diff_attn",
    )(p, p, p, cq, saq, sbq, ck, sak, sbk, diff_lam, subln_g.reshape(1, LANES))


def _merge_kernel(hs_ref, gate_ref, ga_ref, gb_ref, ao_ref, x_ref, m_ref, g2_ref, wl_ref, wa_ref, wo_ref,
                  xo_ref, hp_ref):
    lru_in = (hs_ref[...] * _gelu(gate_ref[0].astype(F32))).astype(BF16)
    ya = jnp.dot(lru_in, wl_ref[...], preferred_element_type=F32)
    yb = jnp.dot(ao_ref[0], wa_ref[...], preferred_element_type=F32)
    mix = _sigmoid(ga_ref[0].astype(F32)) * ya + _sigmoid(gb_ref[0].astype(F32)) * yb
    out = jnp.dot(mix.astype(BF16), wo_ref[...], preferred_element_type=F32)
    xn = x_ref[0] + m_ref[0, 0, 2:3, :] * out
    xo_ref[0] = xn
    hp_ref[0] = _rms_mod(xn, g2_ref[...], m_ref[0, 0, 3:4, :], m_ref[0, 0, 4:5, :]).astype(hp_ref.dtype)


def _merge(hsum_tm, p, ao, x, mods, g2, wl, wa, wo, ctx_len, with_ctx):
    b, t, d = x.shape
    tm = TOK_BLK
    off = 0 if with_ctx else ctx_len // tm
    n = t // tm - off
    seg = (lambda i: jnp.minimum(i, 1)) if with_ctx else (lambda i: 1)
    wspec = pl.BlockSpec((d, d), lambda bi, i: (0, 0))
    return pl.pallas_call(
        _merge_kernel,
        out_shape=(jax.ShapeDtypeStruct((b, n * tm, d), F32), jax.ShapeDtypeStruct((b, n * tm, d), BF16)),
        grid=(b, n),
        in_specs=[
            pl.BlockSpec((tm, d), lambda bi, i: (i + off, bi)),
            pl.BlockSpec((1, tm, d), lambda bi, i: (bi, i + off, 0)),
            pl.BlockSpec((1, tm, d), lambda bi, i: (bi, i + off, 4)),
            pl.BlockSpec((1, tm, d), lambda bi, i: (bi, i + off, 5)),
            pl.BlockSpec((1, tm, d), lambda bi, i: (bi, i, 0)),
            pl.BlockSpec((1, tm, d), lambda bi, i: (bi, i + off, 0)),
            pl.BlockSpec((1, 1, 6, d), lambda bi, i: (bi, seg(i), 0, 0)),
            pl.BlockSpec((1, d), lambda bi, i: (0, 0)),
            wspec, wspec, wspec,
        ],
        out_specs=(pl.BlockSpec((1, tm, d), lambda bi, i: (bi, i, 0)),
                   pl.BlockSpec((1, tm, d), lambda bi, i: (bi, i, 0))),
        compiler_params=_cparams(("parallel", "parallel")),
        name="merge_out",
    )(hsum_tm, p, p, p, ao, x, mods, g2.reshape(1, d), wl, wa, wo)


def _sort_network(n):
    pairs = []
    p = 1
    while p < n:
        k = p
        while k >= 1:
            j = k % p
            while j <= n - 1 - k:
                for i in range(min(k - 1, n - j - k - 1) + 1):
                    if (i + j) // (2 * p) == (i + j + k) // (2 * p):
                        pairs.append((i + j, i + j + k))
                j += 2 * k
            k //= 2
        p *= 2
    return pairs


def _sorted_desc(vals):
    vals = list(vals)
    for i, j in _sort_network(len(vals)):
        vals[i], vals[j] = jnp.maximum(vals[i], vals[j]), jnp.minimum(vals[i], vals[j])
    return vals


def _merge_top(a, b):
    n = len(a)
    z = [jnp.maximum(a[j], b[n - 1 - j]) for j in range(n)]
    d = n // 2
    while d >= 1:
        for j in range(n):
            if not j & d:
                z[j], z[j + d] = jnp.maximum(z[j], z[j + d]), jnp.minimum(z[j], z[j + d])
        d //= 2
    return z


def _count_leading(test, pivots):
    v = pivots
    b8 = test(v[7])
    b4 = test(jnp.where(b8, v[11], v[3]))
    b2 = test(jnp.where(b8, jnp.where(b4, v[13], v[9]), jnp.where(b4, v[5], v[1])))
    b1 = test(jnp.where(b8, jnp.where(b4, jnp.where(b2, v[14], v[12]), jnp.where(b2, v[10], v[8])),
                        jnp.where(b4, jnp.where(b2, v[6], v[4]), jnp.where(b2, v[2], v[0]))))
    lo = (jnp.where(b8, 8.0, 0.0) + jnp.where(b4, 4.0, 0.0)) + (jnp.where(b2, 2.0, 0.0) + jnp.where(b1, 1.0, 0.0))
    return jnp.where(test(v[15]), 16.0, lo)


def _peer_kernel(hp_ref, wqt_ref, kkt_ref, u_ref, vt_ref, y_ref,
                 hpt_s, qt_s, st_s, nbf_s, e1f_s, r2f_s, e2f_s, nb_s, e1_s, r2_s, e2_s, g0_s, g1_s, acc_s,
                 *, n_chunk, n_steps):
    step = pl.program_id(0)
    c = step % n_chunk
    cy = (step + n_chunk - 1) % n_chunk
    tm = hp_ref.shape[0]
    kk = PEER_TOPK
    pk = PEER_PACK

    @pl.when(step == 0)
    def _():
        g1_s[...] = jnp.zeros_like(g1_s)
        acc_s[...] = jnp.zeros_like(acc_s)

    @pl.when((c == 0) & (step < n_steps - 1))
    def _():
        hpt = hp_ref[...].astype(F32).T.astype(BF16)
        hpt_s[...] = hpt
        qt_s[...] = jnp.dot(wqt_ref[...], hpt, preferred_element_type=F32)
        n_tt = tm // LANES
        per = 8 // n_tt

        def head_group(grp, carry):
            for m in range(8):
                h = grp * per + m // n_tt
                tt = m % n_tt
                qh = qt_s[pl.ds(pl.multiple_of(h * LANES, LANES), LANES), tt * LANES:(tt + 1) * LANES]
                st_s[m * SC_PITCH:m * SC_PITCH + 2 * N_KEYS, :] = jnp.dot(
                    kkt_ref[h], qh.astype(BF16), preferred_element_type=F32)

            def key(k):
                return st_s[pl.ds(k, 8, stride=SC_PITCH), :]

            def top(first):
                t = _sorted_desc([key(first + k) for k in range(kk)])
                for g in range(1, N_KEYS // kk):
                    t = _merge_top(t, _sorted_desc([key(first + g * kk + k) for k in range(kk)]))
                return t

            v1 = top(0)
            v2 = top(N_KEYS)
            rows = [[v1[a] + v2[b] for b in range(kk // (a + 1))] for a in range(kk)]
            neg = jnp.full((8, LANES), NEG, F32)
            best = rows[0]
            best = _merge_top(best, rows[1] + [neg] * (kk - len(rows[1])))
            mid = [x for r in rows[2:7] for x in r]
            best = _merge_top(best, _sorted_desc(mid + [neg] * (kk - len(mid))))
            tail = [x for r in rows[7:] for x in r]
            best = _merge_top(best, _sorted_desc(tail + [neg] * (kk - len(tail))))
            tau = best[kk - 1]
            z = None
            for x in best:
                e = jnp.exp(x - best[0])
                z = e if z is None else z + e
            rz = 1.0 / z
            for k in range(N_KEYS):
                s1 = key(k)
                nb = _count_leading(lambda p: s1 + p >= tau, v2)
                nbf_s[pl.ds(k, 8, stride=TB_PITCH), :] = jnp.where(s1 >= v1[kk - 1], nb, 0.0)
                e1f_s[pl.ds(k, 8, stride=TB_PITCH), :] = jnp.exp(s1 - v1[0])
                s2 = key(N_KEYS + k)
                r2f_s[pl.ds(k, 8, stride=TB_PITCH), :] = _count_leading(lambda p: p > s2, v2)
                e2f_s[pl.ds(k, 8, stride=TB_PITCH), :] = jnp.exp(s2 - v2[0]) * rz
            for m in range(8):
                h = grp * per + m // n_tt
                lanes = slice((m % n_tt) * LANES, (m % n_tt + 1) * LANES)
                rws = slice(m * TB_PITCH, m * TB_PITCH + N_KEYS)
                nb_s[h, :, lanes] = nbf_s[rws, :]
                e1_s[h, :, lanes] = e1f_s[rws, :]
                r2_s[h, :, :, lanes] = r2f_s[rws, :].astype(BF16).reshape(N_KEYS // pk, pk, LANES)
                e2_s[h, :, :, lanes] = e2f_s[rws, :].astype(BF16).reshape(N_KEYS // pk, pk, LANES)
            return carry

        lax.fori_loop(0, PEER_HEADS // per, head_group, 0)

    sub = PEER_SUB
    n_per = sub // N_KEYS
    n_sub = u_ref.shape[0] // sub
    zero = jnp.zeros((), BF16)
    keep = jnp.where(cy == 0, 0.0, 1.0).astype(F32)

    half = sub // 2
    n_here = half // N_KEYS

    def main(g_read, g_write):
        hpt = hpt_s[...]

        def a_piece(j, p):
            r0 = j * sub + p * half
            return jnp.dot(u_ref[r0:r0 + half, :], hpt, preferred_element_type=F32)

        def gate_piece(a, j, p, after):
            zrow = 0.0
            if after is not None:
                bits = pltpu.bitcast(after[after.shape[0] - 8:, :], jnp.uint32)
                bits = lax.shift_right_logical(lax.shift_right_logical(bits, jnp.uint32(16)), jnp.uint32(16))
                zrow = pltpu.bitcast(bits, F32)[0:1, :]
            act = _gelu(a.astype(BF16)).reshape(n_here, N_KEYS // pk, pk, tm)
            for s in range(n_here):
                i1 = c * (u_ref.shape[0] // N_KEYS) + j * n_per + p * n_here + s
                w = None
                for h in range(PEER_HEADS):
                    nbb = jnp.broadcast_to(nb_s[h, pl.ds(i1, 1), :] + zrow, (pk, tm)).astype(BF16)
                    e1b = jnp.broadcast_to(e1_s[h, pl.ds(i1, 1), :], (pk, tm)).astype(BF16)
                    term = jnp.where(r2_s[h] < nbb[None], e2_s[h], zero) * e1b[None]
                    w = term if w is None else w + term
                r0 = j * sub + p * half + s * N_KEYS
                g_write[r0:r0 + N_KEYS, :] = (act[s] * w).reshape(N_KEYS, tm)

        def y_slice(j):
            return jnp.dot(vt_ref[:, j * sub:(j + 1) * sub], g_read[j * sub:(j + 1) * sub, :],
                           preferred_element_type=F32)

        a_cur = [a_piece(0, 0), a_piece(0, 1)]
        y = None
        for j in range(n_sub):
            a_next = [None, None]
            for p in range(2):
                if j + 1 < n_sub:
                    a_next[p] = a_piece(j + 1, p)
                gate_piece(a_cur[p], j, p, a_next[p])
            yj = y_slice(j)
            y = yj if y is None else y + yj
            a_cur = a_next
        acc_s[...] = acc_s[...] * keep + y

    @pl.when(step % 2 == 0)
    def _():
        main(g1_s, g0_s)

    @pl.when(step % 2 == 1)
    def _():
        main(g0_s, g1_s)

    @pl.when((cy == n_chunk - 1) & (step > 0))
    def _():
        y_ref[...] = acc_s[...].T


def _peer(hp, wqt, kkt, u, vt):
    n, d = hp.shape
    e = u.shape[0]
    tm, ec = PEER_TOK, PEER_EC
    n_blk, n_chunk = n // tm, e // ec
    n_steps = n_blk * n_chunk + 1
    return pl.pallas_call(
        functools.partial(_peer_kernel, n_chunk=n_chunk, n_steps=n_steps),
        out_shape=jax.ShapeDtypeStruct((n, d), F32),
        grid=(n_steps,),
        in_specs=[
            pl.BlockSpec((tm, d), lambda s: (jnp.minimum(s // n_chunk, n_blk - 1), 0)),
            pl.BlockSpec((d, d), lambda s: (0, 0)),
            pl.BlockSpec((PEER_HEADS, 2 * N_KEYS, LANES), lambda s: (0, 0, 0)),
            pl.BlockSpec((ec, d), lambda s: (s % n_chunk, 0)),
            pl.BlockSpec((d, ec), lambda s: (0, (s + n_chunk - 1) % n_chunk)),
        ],
        out_specs=pl.BlockSpec((tm, d), lambda s: (jnp.maximum(s - 1, 0) // n_chunk, 0)),
        scratch_shapes=[
            pltpu.VMEM((d, tm), BF16),
            pltpu.VMEM((d, tm), F32),
            pltpu.VMEM((8 * SC_PITCH, LANES), F32),
            pltpu.VMEM((8 * TB_PITCH, LANES), F32),
            pltpu.VMEM((8 * TB_PITCH, LANES), F32),
            pltpu.VMEM((8 * TB_PITCH, LANES), F32),
            pltpu.VMEM((8 * TB_PITCH, LANES), F32),
            pltpu.VMEM((PEER_HEADS, N_KEYS, tm), F32),
            pltpu.VMEM((PEER_HEADS, N_KEYS, tm), F32),
            pltpu.VMEM((PEER_HEADS, N_KEYS // PEER_PACK, PEER_PACK, tm), BF16),
            pltpu.VMEM((PEER_HEADS, N_KEYS // PEER_PACK, PEER_PACK, tm), BF16),
            pltpu.VMEM((ec, tm), BF16),
            pltpu.VMEM((ec, tm), BF16),
            pltpu.VMEM((d, tm), F32),
        ],
        compiler_params=_cparams(("arbitrary",)),
        name="peer",
    )(hp, wqt, kkt, u, vt)


def _resid_norm_kernel(x_ref, y_ref, m_ref, mn_ref, g_ref, xo_ref, hn_ref):
    xn = x_ref[0] + m_ref[0, 0, 5:6, :] * y_ref[0]
    xo_ref[0] = xn
    hn_ref[0] = _rms_mod(xn, g_ref[...], mn_ref[0, 0, 0:1, :], mn_ref[0, 0, 1:2, :]).astype(hn_ref.dtype)


def _resid_final_kernel(x_ref, y_ref, m_ref, g_ref, o_ref):
    xn = x_ref[0] + m_ref[0, 0, 5:6, :] * y_ref[0]
    o_ref[0] = xn * lax.rsqrt(jnp.mean(xn * xn, axis=-1, keepdims=True) + EPS) * g_ref[...]


def _resid_norm(x, y, mods, mods_next, g_next):
    b, t, d = x.shape
    tm = TOK_BLK
    xspec = pl.BlockSpec((1, tm, d), lambda bi, i: (bi, i, 0))
    mspec = pl.BlockSpec((1, 1, 6, d), lambda bi, i: (bi, jnp.minimum(i, 1), 0, 0))
    return pl.pallas_call(
        _resid_norm_kernel,
        out_shape=(jax.ShapeDtypeStruct((b, t, d), F32), jax.ShapeDtypeStruct((b, t, d), BF16)),
        grid=(b, t // tm),
        in_specs=[xspec, xspec, mspec, mspec, pl.BlockSpec((1, d), lambda bi, i: (0, 0))],
        out_specs=(xspec, xspec),
        compiler_params=_cparams(("parallel", "parallel")),
        name="resid_norm",
    )(x, y, mods, mods_next, g_next.reshape(1, d))


def _resid_final(x, y, mods, g):
    b, t, d = x.shape
    tm = TOK_BLK
    xspec = pl.BlockSpec((1, tm, d), lambda bi, i: (bi, i, 0))
    return pl.pallas_call(
        _resid_final_kernel,
        out_shape=jax.ShapeDtypeStruct((b, t, d), F32),
        grid=(b, t // tm),
        in_specs=[xspec, xspec, pl.BlockSpec((1, 1, 6, d), lambda bi, i: (bi, 1, 0, 0)),
                  pl.BlockSpec((1, d), lambda bi, i: (0, 0))],
        out_specs=xspec,
        compiler_params=_cparams(("parallel", "parallel")),
        name="resid_final",
    )(x, y, mods, g.reshape(1, d))


def _rope_tables(n_latent, ctx_len):
    rows = n_latent // GRID_W
    row, col = jnp.meshgrid(jnp.arange(rows), jnp.arange(GRID_W), indexing='ij')
    pos = jnp.stack([row.reshape(-1), col.reshape(-1)], axis=-1).astype(F32)
    inv = ROPE_THETA ** (-jnp.arange(ROPE_FREQS, dtype=F32) / ROPE_FREQS)
    ang = pos[:, :, None] * inv
    cos, sin = jnp.cos(ang), jnp.sin(ang)
    lane = jnp.arange(LANES)
    axis = (lane % ATTN_DH) // (2 * ROPE_FREQS)
    freq = lane % ROPE_FREQS
    second = (lane % (2 * ROPE_FREQS)) >= ROPE_FREQS
    c = cos[:, axis, freq]
    s = sin[:, axis, freq]
    sa = jnp.where(second, 0.0, -s)
    sb = jnp.where(second, s, 0.0)
    ones = jnp.ones((ctx_len, LANES), F32)
    zeros = jnp.zeros((ctx_len, LANES), F32)
    ck = jnp.concatenate([ones, c], axis=0)
    sak = jnp.concatenate([zeros, sa], axis=0)
    sbk = jnp.concatenate([zeros, sb], axis=0)
    scale = ATTN_DH ** -0.5 * math.log2(math.e)
    return ck * scale, sak * scale, sbk * scale, ck, sak, sbk


def _block_diag_pairs(w):
    lead = w.shape[:-3]
    w = w.reshape(lead + (LRU_BLOCKS // 2, 2, LRU_BW, LRU_BW))
    z = jnp.zeros_like(w[..., 0, :, :])
    top = jnp.concatenate([w[..., 0, :, :], z], axis=-1)
    bot = jnp.concatenate([z, w[..., 1, :, :]], axis=-1)
    return jnp.concatenate([top, bot], axis=-2)


def _peer_key_tiles(keys):
    z = jnp.zeros_like(keys[:, 0])
    top = jnp.concatenate([keys[:, 0], z], axis=-1)
    bot = jnp.concatenate([z, keys[:, 1]], axis=-1)
    return jnp.concatenate([top, bot], axis=1)


def kernel(x, c, ctx, c_ctx, mod_w, mod_b, norm1_g, norm2_g, w_in, conv_w, conv_b, lru_w, lru_b, lru_lam,
           diff_lam, subln_g, w_br_lru, w_br_attn, w_out, peer_wq, peer_keys, peer_u, peer_v, final_g):
    b, seq, d = x.shape
    ctx_len = ctx.shape[1]
    depth = mod_w.shape[0]
    t = ctx_len + seq

    cc = jnp.concatenate([c, c_ctx[None, :], jnp.zeros((16 - b - 1, d), F32)], axis=0)
    m = _mods(cc, mod_w, mod_b)
    m_lat = m[:, :b].reshape(depth, b, 1, 6, d)
    m_ctx = jnp.broadcast_to(m[:, b].reshape(depth, 1, 1, 6, d), (depth, b, 1, 6, d))
    mods = jnp.concatenate([m_ctx, m_lat], axis=2)

    tabs = _rope_tables(seq, ctx_len)
    xs = jnp.concatenate([ctx, x], axis=1)
    hn = _norm_mod(xs, norm1_g[0], mods[0])

    for li in range(depth):
        last = li == depth - 1
        lambda_init = 0.8 - 0.6 * math.exp(-0.3 * li)
        w_in_b = w_in[li].astype(BF16)
        u_tm = _matmul_time_major(hn, w_in_b[:, :d], F32)
        p = _matmul(hn, w_in_b[:, d:], BF16)
        hsum = _lru(u_tm.reshape(t, b, d), conv_w[li], conv_b[li],
                    _block_diag_pairs(lru_w[li]).astype(BF16), lru_b[li], lru_lam[li], ctx_len)
        ao = _attention(p, tabs, diff_lam[li], subln_g[li], lambda_init, ctx_len, not last)
        xm, hp = _merge(hsum.reshape(t, b * d), p, ao, xs, mods[li], norm2_g[li],
                        w_br_lru[li].astype(BF16), w_br_attn[li].astype(BF16), w_out[li].astype(BF16),
                        ctx_len, not last)
        nt = xm.shape[1]
        y = _peer(hp.reshape(b * nt, d), peer_wq[li].T.astype(BF16),
                  _peer_key_tiles(peer_keys[li]).astype(BF16),
                  peer_u[li].astype(BF16), peer_v[li].T.astype(BF16)).reshape(b, nt, d)
        if last:
            return _resid_final(xm, y, mods[li], final_g)
        xs, hn = _resid_norm(xm, y, mods[li], mods[li + 1], norm1_g[li + 1])
```

```python
import functools
import math

import jax
import jax.numpy as jnp
from jax import lax
from jax.experimental import pallas as pl
from jax.experimental.pallas import tpu as pltpu

F32 = jnp.float32
BF16 = jnp.bfloat16

D_MODEL = 1024
GRID_W = 64
EPS = 1e-6
LRU_BLOCKS = 16
LRU_BW = D_MODEL // LRU_BLOCKS
CONV_W = 4
CONV_LEFT = 2
LRU_C = 8.0
ATTN_HEADS = 8
ATTN_DH = 64
ROPE_THETA = 10000.0
ROPE_FREQS = ATTN_DH // 4
PEER_HEADS = 8
N_KEYS = 128
PEER_TOPK = 16

LANES = 128
TOK_BLK = 256
MM_ROWS = 768
LRU_CHUNK = 256
ATTN_HPS = 4
PEER_TOK = 512
PEER_EC = 2048
PEER_SUB = 512
PEER_PACK = 16
SC_PITCH = 260
TB_PITCH = 132
NEG = -1e30
VMEM_LIMIT = 56 * 1024 * 1024


def _cparams(sem):
    return pltpu.CompilerParams(dimension_semantics=sem, vmem_limit_bytes=VMEM_LIMIT)


def _gelu(x):
    return 0.5 * x * (1.0 + jnp.tanh(0.7978845608028654 * (x + 0.044715 * (x * x * x))))


def _sigmoid(x):
    return 1.0 / (1.0 + jnp.exp(-x))


def _rms_mod(x, g, shift, scale):
    y = x * lax.rsqrt(jnp.mean(x * x, axis=-1, keepdims=True) + EPS)
    return (y * g) * (1.0 + scale) + shift


def _mods_kernel(c_ref, w_ref, b_ref, o_ref):
    c = c_ref[...]
    s = c * _sigmoid(c)
    o_ref[0] = jnp.dot(s, w_ref[0], preferred_element_type=F32,
                       precision=lax.Precision.HIGHEST) + b_ref[0]


def _mods(cc, mod_w, mod_b):
    depth, d, n = mod_w.shape
    tn = 1024
    return pl.pallas_call(
        _mods_kernel,
        out_shape=jax.ShapeDtypeStruct((depth, cc.shape[0], n), F32),
        grid=(depth, n // tn),
        in_specs=[
            pl.BlockSpec((cc.shape[0], d), lambda l, j: (0, 0)),
            pl.BlockSpec((1, d, tn), lambda l, j: (l, 0, j)),
            pl.BlockSpec((1, 1, tn), lambda l, j: (l, 0, j)),
        ],
        out_specs=pl.BlockSpec((1, cc.shape[0], tn), lambda l, j: (l, 0, j)),
        compiler_params=_cparams(("parallel", "parallel")),
        name="mods",
    )(cc, mod_w, mod_b.reshape(depth, 1, n))


def _norm_mod_kernel(x_ref, g_ref, m_ref, o_ref):
    o_ref[0] = _rms_mod(x_ref[0], g_ref[...], m_ref[0, 0, 0:1, :], m_ref[0, 0, 1:2, :]).astype(o_ref.dtype)


def _norm_mod(x, g, mods):
    b, t, d = x.shape
    return pl.pallas_call(
        _norm_mod_kernel,
        out_shape=jax.ShapeDtypeStruct((b, t, d), BF16),
        grid=(b, t // TOK_BLK),
        in_specs=[
            pl.BlockSpec((1, TOK_BLK, d), lambda bi, i: (bi, i, 0)),
            pl.BlockSpec((1, d), lambda bi, i: (0, 0)),
            pl.BlockSpec((1, 1, 6, d), lambda bi, i: (bi, jnp.minimum(i, 1), 0, 0)),
        ],
        out_specs=pl.BlockSpec((1, TOK_BLK, d), lambda bi, i: (bi, i, 0)),
        compiler_params=_cparams(("parallel", "parallel")),
        name="norm_mod",
    )(x, g.reshape(1, d), mods)


def _mm_kernel(a_ref, w_ref, o_ref):
    o_ref[0] = jnp.dot(a_ref[0], w_ref[...], preferred_element_type=F32).astype(o_ref.dtype)


def _mm_tm_kernel(a_ref, w_ref, o_ref):
    o_ref[...] = jnp.dot(a_ref[0], w_ref[...], preferred_element_type=F32).astype(o_ref.dtype)


def _mm_rows(t):
    return MM_ROWS if t % MM_ROWS == 0 else TOK_BLK


def _matmul(a, w, out_dtype, col0=0):
    b, t, k = a.shape
    tn = 1024
    n = w.shape[1] - col0
    c0 = col0 // tn
    tm = _mm_rows(t)
    return pl.pallas_call(
        _mm_kernel,
        out_shape=jax.ShapeDtypeStruct((b, t, n), out_dtype),
        grid=(n // tn, b, t // tm),
        in_specs=[
            pl.BlockSpec((1, tm, k), lambda j, bi, i: (bi, i, 0)),
            pl.BlockSpec((k, tn), lambda j, bi, i: (0, j + c0)),
        ],
        out_specs=pl.BlockSpec((1, tm, tn), lambda j, bi, i: (bi, i, j)),
        compiler_params=_cparams(("parallel", "parallel", "parallel")),
        name="matmul",
    )(a, w)


def _matmul_time_major(a, w, n, out_dtype):
    b, t, k = a.shape
    tm = _mm_rows(t)
    return pl.pallas_call(
        _mm_tm_kernel,
        out_shape=jax.ShapeDtypeStruct((t, b * n), out_dtype),
        grid=(b, t // tm),
        in_specs=[
            pl.BlockSpec((1, tm, k), lambda bi, i: (bi, i, 0)),
            pl.BlockSpec((k, n), lambda bi, i: (0, 0)),
        ],
        out_specs=pl.BlockSpec((tm, n), lambda bi, i: (i, bi)),
        compiler_params=_cparams(("parallel", "parallel")),
        name="matmul_time_major",
    )(a, w)


def _lru_kernel(u_ref, cw_ref, cb_ref, w_ref, b_ref, lam_ref, o_ref, af_s, df_s, ab_s, db_s, hf_s, hb_s, *, ctx_len):
    t_total, nb, nc = u_ref.shape
    tc = LRU_CHUNK
    segments = ((0, ctx_len), (ctx_len, t_total))

    def conv_chunk(t0, seg):
        acc = jnp.broadcast_to(cb_ref[...].reshape(1, 1, nc), (tc, nb, nc))
        for k in range(CONV_W):
            lo = t0 + k - CONV_LEFT
            hi = lo + tc
            vlo, vhi = max(lo, seg[0]), min(hi, seg[1])
            piece = u_ref[vlo:vhi]
            if vlo > lo:
                piece = jnp.concatenate([jnp.zeros((vlo - lo, nb, nc), F32), piece], axis=0)
            if vhi < hi:
                piece = jnp.concatenate([piece, jnp.zeros((hi - vhi, nb, nc), F32)], axis=0)
            acc = acc + piece * cw_ref[k:k + 1, :].reshape(1, 1, nc)
        return acc

    def gates(t0, seg, d, a_dst, d_dst):
        x = conv_chunk(t0, seg).reshape(tc * nb, nc)
        xb = x.astype(BF16)
        rec = _sigmoid(jnp.dot(xb, w_ref[d, 0, 0], preferred_element_type=F32) + b_ref[d, 0:1, :])
        inp = _sigmoid(jnp.dot(xb, w_ref[d, 1, 0], preferred_element_type=F32) + b_ref[d, 1:2, :])
        nlam = -lam_ref[d:d + 1, :]
        e = jnp.exp(-jnp.abs(nlam))
        ep1 = 1.0 + e
        sp = jnp.maximum(nlam, 0.0) + jnp.where(ep1 == 1.0, e, jnp.log(ep1) * (e / (ep1 - 1.0)))
        a = jnp.exp2(rec * (sp * (-LRU_C * math.log2(math.e))))
        drive = jnp.sqrt(1.0 - a * a) * (inp * x)
        a_dst[...] = a.reshape(tc, nb, nc)
        d_dst[...] = drive.reshape(tc, nb, nc)

    chunks = [(t0, seg) for seg in segments for t0 in range(seg[0], seg[1], tc)]
    rev = [c for c in reversed(chunks) if c[1] == segments[0]] + [c for c in reversed(chunks) if c[1] == segments[1]]
    written = set()
    hf = jnp.zeros((nb, nc), F32)
    hb = jnp.zeros((nb, nc), F32)
    for (tf, segf), (tb, segb) in zip(chunks, rev):
        gates(tf, segf, 0, af_s, df_s)
        gates(tb, segb, 1, ab_s, db_s)

        def step(i, carry):
            f, r = carry
            f = af_s[i] * f + df_s[i]
            hf_s[i] = f
            j = tc - 1 - i
            r = ab_s[j] * r + db_s[j]
            hb_s[j] = r
            return f, r
        hf, hb = lax.fori_loop(0, tc, step, (hf, hb), unroll=8)
        for t0, buf in ((tf, hf_s), (tb, hb_s)):
            if t0 in written:
                o_ref[t0:t0 + tc] = o_ref[t0:t0 + tc] + buf[...]
            else:
                o_ref[t0:t0 + tc] = buf[...]
                written.add(t0)


def _lru(u, conv_w, conv_b, wbd, lru_b, lru_lam, ctx_len):
    t, b, c = u.shape
    nc = LANES
    return pl.pallas_call(
        functools.partial(_lru_kernel, ctx_len=ctx_len),
        out_shape=jax.ShapeDtypeStruct((t, b, c), F32),
        grid=(c // nc,),
        in_specs=[
            pl.BlockSpec((t, b, nc), lambda j: (0, 0, j)),
            pl.BlockSpec((CONV_W, nc), lambda j: (0, j)),
            pl.BlockSpec((1, nc), lambda j: (0, j)),
            pl.BlockSpec((2, 2, 1, nc, nc), lambda j: (0, 0, j, 0, 0)),
            pl.BlockSpec((2, 2, nc), lambda j: (0, 0, j)),
            pl.BlockSpec((2, nc), lambda j: (0, j)),
        ],
        out_specs=pl.BlockSpec((t, b, nc), lambda j: (0, 0, j)),
        scratch_shapes=[pltpu.VMEM((LRU_CHUNK, b, nc), F32)] * 6,
        compiler_params=_cparams(("parallel",)),
        name="rglru",
    )(u, conv_w, conv_b.reshape(1, c), wbd, lru_b, lru_lam)


def _rope(x, c, sa, sb):
    return x * c + pltpu.roll(x, LANES - ROPE_FREQS, 1) * sa + pltpu.roll(x, ROPE_FREQS, 1) * sb


def _attn_kernel(q_ref, k_ref, v_ref, cq_ref, saq_ref, sbq_ref, ck_ref, sak_ref, sbk_ref, dl_ref, g_ref,
                 o_ref, kr_s, *, lambda_init, ctx_len, q_off):
    qi = pl.program_id(2)
    heads = range(ATTN_HPS)

    @pl.when(qi == 0)
    def _():
        for hh in heads:
            k = k_ref[0, :, hh * LANES:(hh + 1) * LANES].astype(F32)
            kr_s[:, hh * LANES:(hh + 1) * LANES] = _rope(k, ck_ref[...], sak_ref[...], sbk_ref[...]).astype(BF16)

    dl = dl_ref[...]
    lam = (jnp.exp(jnp.sum(dl[0:1] * dl[1:2], axis=-1, keepdims=True))
           - jnp.exp(jnp.sum(dl[2:3] * dl[3:4], axis=-1, keepdims=True)) + lambda_init)
    nt = (((1,), (1,)), ((), ()))
    qm = []
    for hh in heads:
        q = _rope(q_ref[0, :, hh * LANES:(hh + 1) * LANES].astype(F32), cq_ref[...], saq_ref[...], sbq_ref[...])
        lane = lax.broadcasted_iota(jnp.int32, q.shape, 1)
        qm.append((jnp.where(lane < ATTN_DH, q, 0.0).astype(BF16), jnp.where(lane >= ATTN_DH, q, 0.0).astype(BF16)))

    def attend(nk):
        scores = []
        for hh in heads:
            kr = kr_s[0:nk, hh * LANES:(hh + 1) * LANES]
            both = lax.dot_general(jnp.concatenate(qm[hh], axis=0), kr, nt, preferred_element_type=F32)
            n_q = qm[hh][0].shape[0]
            scores.append([both[0:n_q], both[n_q:2 * n_q]])
        for hh in heads:
            ps = []
            for s in scores[hh]:
                p = jnp.exp2(s - jnp.max(s, axis=-1, keepdims=True))
                ps.append((p, jnp.sum(p, axis=-1, keepdims=True)))
            (p0, l0), (p1, l1) = ps
            w = p0 * (1.0 / l0) - p1 * (lam / l1)
            o = jnp.dot(w.astype(BF16), v_ref[0, 0:nk, hh * LANES:(hh + 1) * LANES], preferred_element_type=F32)
            o = o * lax.rsqrt(jnp.mean(o * o, axis=-1, keepdims=True) + EPS)
            o_ref[0, :, hh * LANES:(hh + 1) * LANES] = (o * (g_ref[...] * (1.0 - lambda_init))).astype(o_ref.dtype)

    nk_all = k_ref.shape[1]
    if q_off == 0:
        @pl.when(qi == 0)
        def _():
            attend(ctx_len)

        @pl.when(qi > 0)
        def _():
            attend(nk_all)
    else:
        attend(nk_all)


def _attention(p, tabs, diff_lam, subln_g, lambda_init, ctx_len, with_ctx):
    b, t, _ = p.shape
    tq = TOK_BLK
    q_off = 0 if with_ctx else ctx_len // tq
    nq = t // tq - q_off
    cq, saq, sbq, ck, sak, sbk = tabs
    wb = ATTN_HPS * LANES
    hb = D_MODEL // wb
    qspec = pl.BlockSpec((tq, LANES), lambda bi, h, i: (i + q_off, 0))
    kspec = pl.BlockSpec((t, LANES), lambda bi, h, i: (0, 0))
    return pl.pallas_call(
        functools.partial(_attn_kernel, lambda_init=lambda_init, ctx_len=ctx_len, q_off=q_off),
        out_shape=jax.ShapeDtypeStruct((b, nq * tq, D_MODEL), BF16),
        grid=(b, ATTN_HEADS // ATTN_HPS, nq),
        in_specs=[
            pl.BlockSpec((1, tq, wb), lambda bi, h, i: (bi, i + q_off, hb + h)),
            pl.BlockSpec((1, t, wb), lambda bi, h, i: (bi, 0, 2 * hb + h)),
            pl.BlockSpec((1, t, wb), lambda bi, h, i: (bi, 0, 3 * hb + h)),
            qspec, qspec, qspec, kspec, kspec, kspec,
            pl.BlockSpec((4, ATTN_DH), lambda bi, h, i: (0, 0)),
            pl.BlockSpec((1, LANES), lambda bi, h, i: (0, 0)),
        ],
        out_specs=pl.BlockSpec((1, tq, wb), lambda bi, h, i: (bi, i, h)),
        scratch_shapes=[pltpu.VMEM((t, wb), BF16)],
        compiler_params=_cparams(("parallel", "parallel", "arbitrary")),
        name="diff_attn",
    )(p, p, p, cq, saq, sbq, ck, sak, sbk, diff_lam, subln_g.reshape(1, LANES))


def _merge_kernel(hs_ref, gate_ref, ga_ref, gb_ref, ao_ref, x_ref, m_ref, g2_ref, wl_ref, wa_ref, wo_ref,
                  xo_ref, hp_ref):
    lru_in = (hs_ref[...] * _gelu(gate_ref[0].astype(F32))).astype(BF16)
    ya = jnp.dot(lru_in, wl_ref[...], preferred_element_type=F32)
    yb = jnp.dot(ao_ref[0], wa_ref[...], preferred_element_type=F32)
    mix = _sigmoid(ga_ref[0].astype(F32)) * ya + _sigmoid(gb_ref[0].astype(F32)) * yb
    out = jnp.dot(mix.astype(BF16), wo_ref[...], preferred_element_type=F32)
    xn = x_ref[0] + m_ref[0, 0, 2:3, :] * out
    xo_ref[0] = xn
    hp_ref[0] = _rms_mod(xn, g2_ref[...], m_ref[0, 0, 3:4, :], m_ref[0, 0, 4:5, :]).astype(hp_ref.dtype)


def _merge(hsum_tm, p, ao, x, mods, g2, wl, wa, wo, ctx_len, with_ctx):
    b, t, d = x.shape
    tm = TOK_BLK
    off = 0 if with_ctx else ctx_len // tm
    n = t // tm - off
    seg = (lambda i: jnp.minimum(i, 1)) if with_ctx else (lambda i: 1)
    wspec = pl.BlockSpec((d, d), lambda bi, i: (0, 0))
    return pl.pallas_call(
        _merge_kernel,
        out_shape=(jax.ShapeDtypeStruct((b, n * tm, d), F32), jax.ShapeDtypeStruct((b, n * tm, d), BF16)),
        grid=(b, n),
        in_specs=[
            pl.BlockSpec((tm, d), lambda bi, i: (i + off, bi)),
            pl.BlockSpec((1, tm, d), lambda bi, i: (bi, i + off, 0)),
            pl.BlockSpec((1, tm, d), lambda bi, i: (bi, i + off, 4)),
            pl.BlockSpec((1, tm, d), lambda bi, i: (bi, i + off, 5)),
            pl.BlockSpec((1, tm, d), lambda bi, i: (bi, i, 0)),
            pl.BlockSpec((1, tm, d), lambda bi, i: (bi, i + off, 0)),
            pl.BlockSpec((1, 1, 6, d), lambda bi, i: (bi, seg(i), 0, 0)),
            pl.BlockSpec((1, d), lambda bi, i: (0, 0)),
            wspec, wspec, wspec,
        ],
        out_specs=(pl.BlockSpec((1, tm, d), lambda bi, i: (bi, i, 0)),
                   pl.BlockSpec((1, tm, d), lambda bi, i: (bi, i, 0))),
        compiler_params=_cparams(("parallel", "parallel")),
        name="merge_out",
    )(hsum_tm, p, p, p, ao, x, mods, g2.reshape(1, d), wl, wa, wo)


def _sort_network(n):
    pairs = []
    p = 1
    while p < n:
        k = p
        while k >= 1:
            j = k % p
            while j <= n - 1 - k:
                for i in range(min(k - 1, n - j - k - 1) + 1):
                    if (i + j) // (2 * p) == (i + j + k) // (2 * p):
                        pairs.append((i + j, i + j + k))
                j += 2 * k
            k //= 2
        p *= 2
    return pairs


def _sorted_desc(vals):
    vals = list(vals)
    for i, j in _sort_network(len(vals)):
        vals[i], vals[j] = jnp.maximum(vals[i], vals[j]), jnp.minimum(vals[i], vals[j])
    return vals


def _merge_top(a, b):
    n = len(a)
    z = [jnp.maximum(a[j], b[n - 1 - j]) for j in range(n)]
    d = n // 2
    while d >= 1:
        for j in range(n):
            if not j & d:
                z[j], z[j + d] = jnp.maximum(z[j], z[j + d]), jnp.minimum(z[j], z[j + d])
        d //= 2
    return z


def _count_leading(test, pivots):
    v = pivots
    b8 = test(v[7])
    b4 = test(jnp.where(b8, v[11], v[3]))
    b2 = test(jnp.where(b8, jnp.where(b4, v[13], v[9]), jnp.where(b4, v[5], v[1])))
    b1 = test(jnp.where(b8, jnp.where(b4, jnp.where(b2, v[14], v[12]), jnp.where(b2, v[10], v[8])),
                        jnp.where(b4, jnp.where(b2, v[6], v[4]), jnp.where(b2, v[2], v[0]))))
    lo = (jnp.where(b8, 8.0, 0.0) + jnp.where(b4, 4.0, 0.0)) + (jnp.where(b2, 2.0, 0.0) + jnp.where(b1, 1.0, 0.0))
    return jnp.where(test(v[15]), 16.0, lo)


def _peer_kernel(hp_ref, wqt_ref, kkt_ref, u_ref, vt_ref, y_ref,
                 hpt_s, qt_s, st_s, nbf_s, e1f_s, r2f_s, e2f_s, nb_s, e1_s, r2_s, e2_s, g0_s, g1_s, acc_s,
                 *, n_chunk, n_steps):
    step = pl.program_id(0)
    c = step % n_chunk
    cy = (step + n_chunk - 1) % n_chunk
    tm = hp_ref.shape[0]
    kk = PEER_TOPK
    pk = PEER_PACK

    @pl.when(step == 0)
    def _():
        g1_s[...] = jnp.zeros_like(g1_s)
        acc_s[...] = jnp.zeros_like(acc_s)

    @pl.when((c == 0) & (step < n_steps - 1))
    def _():
        hpt = hp_ref[...].astype(F32).T.astype(BF16)
        hpt_s[...] = hpt
        qt_s[...] = jnp.dot(wqt_ref[...], hpt, preferred_element_type=F32)
        n_tt = tm // LANES
        per = 8 // n_tt

        def head_group(grp, carry):
            for m in range(8):
                h = grp * per + m // n_tt
                tt = m % n_tt
                qh = qt_s[pl.ds(pl.multiple_of(h * LANES, LANES), LANES), tt * LANES:(tt + 1) * LANES]
                st_s[m * SC_PITCH:m * SC_PITCH + 2 * N_KEYS, :] = jnp.dot(
                    kkt_ref[h], qh.astype(BF16), preferred_element_type=F32)

            def key(k):
                return st_s[pl.ds(k, 8, stride=SC_PITCH), :]

            def top(first):
                t = _sorted_desc([key(first + k) for k in range(kk)])
                for g in range(1, N_KEYS // kk):
                    t = _merge_top(t, _sorted_desc([key(first + g * kk + k) for k in range(kk)]))
                return t

            v1 = top(0)
            v2 = top(N_KEYS)
            rows = [[v1[a] + v2[b] for b in range(kk // (a + 1))] for a in range(kk)]
            neg = jnp.full((8, LANES), NEG, F32)
            best = rows[0]
            best = _merge_top(best, rows[1] + [neg] * (kk - len(rows[1])))
            mid = [x for r in rows[2:7] for x in r]
            best = _merge_top(best, _sorted_desc(mid + [neg] * (kk - len(mid))))
            tail = [x for r in rows[7:] for x in r]
            best = _merge_top(best, _sorted_desc(tail + [neg] * (kk - len(tail))))
            tau = best[kk - 1]
            z = None
            for x in best:
                e = jnp.exp(x - best[0])
                z = e if z is None else z + e
            rz = 1.0 / z
            for k in range(N_KEYS):
                s1 = key(k)
                nb = _count_leading(lambda p: s1 + p >= tau, v2)
                nbf_s[pl.ds(k, 8, stride=TB_PITCH), :] = jnp.where(s1 >= v1[kk - 1], nb, 0.0)
                e1f_s[pl.ds(k, 8, stride=TB_PITCH), :] = jnp.exp(s1 - v1[0])
                s2 = key(N_KEYS + k)
                r2f_s[pl.ds(k, 8, stride=TB_PITCH), :] = _count_leading(lambda p: p > s2, v2)
                e2f_s[pl.ds(k, 8, stride=TB_PITCH), :] = jnp.exp(s2 - v2[0]) * rz
            for m in range(8):
                h = grp * per + m // n_tt
                lanes = slice((m % n_tt) * LANES, (m % n_tt + 1) * LANES)
                rws = slice(m * TB_PITCH, m * TB_PITCH + N_KEYS)
                nb_s[h, :, lanes] = nbf_s[rws, :]
                e1_s[h, :, lanes] = e1f_s[rws, :]
                r2_s[h, :, :, lanes] = r2f_s[rws, :].astype(BF16).reshape(N_KEYS // pk, pk, LANES)
                e2_s[h, :, :, lanes] = e2f_s[rws, :].astype(BF16).reshape(N_KEYS // pk, pk, LANES)
            return carry

        lax.fori_loop(0, PEER_HEADS // per, head_group, 0)

    sub = PEER_SUB
    n_per = sub // N_KEYS
    n_sub = u_ref.shape[0] // sub
    zero = jnp.zeros((), BF16)
    keep = jnp.where(cy == 0, 0.0, 1.0).astype(F32)

    half = sub // 2
    n_here = half // N_KEYS

    def main(g_read, g_write):
        hpt = hpt_s[...]

        def a_piece(j, p):
            r0 = j * sub + p * half
            return jnp.dot(u_ref[r0:r0 + half, :], hpt, preferred_element_type=F32)

        def gate_piece(a, j, p):
            act = _gelu(a.astype(BF16)).reshape(n_here, N_KEYS // pk, pk, tm)
            for s in range(n_here):
                i1 = c * (u_ref.shape[0] // N_KEYS) + j * n_per + p * n_here + s
                w = None
                for h in range(PEER_HEADS):
                    nbb = jnp.broadcast_to(nb_s[h, pl.ds(i1, 1), :], (pk, tm)).astype(BF16)
                    e1b = jnp.broadcast_to(e1_s[h, pl.ds(i1, 1), :], (pk, tm)).astype(BF16)
                    term = jnp.where(r2_s[h] < nbb[None], e2_s[h], zero) * e1b[None]
                    w = term if w is None else w + term
                r0 = j * sub + p * half + s * N_KEYS
                g_write[r0:r0 + N_KEYS, :] = (act[s] * w).reshape(N_KEYS, tm)

        def y_slice(j):
            return jnp.dot(vt_ref[:, j * sub:(j + 1) * sub], g_read[j * sub:(j + 1) * sub, :],
                           preferred_element_type=F32)

        a_cur = [a_piece(0, 0), a_piece(0, 1)]
        y = None
        for j in range(n_sub):
            a_next = [None, None]
            for p in range(2):
                if j + 1 < n_sub:
                    a_next[p] = a_piece(j + 1, p)
                gate_piece(a_cur[p], j, p)
            yj = y_slice(j)
            y = yj if y is None else y + yj
            a_cur = a_next
        acc_s[...] = acc_s[...] * keep + y

    @pl.when(step % 2 == 0)
    def _():
        main(g1_s, g0_s)

    @pl.when(step % 2 == 1)
    def _():
        main(g0_s, g1_s)

    @pl.when((cy == n_chunk - 1) & (step > 0))
    def _():
        y_ref[...] = acc_s[...].T


def _peer(hp, wqt, kkt, u, vt):
    n, d = hp.shape
    e = u.shape[0]
    tm, ec = PEER_TOK, PEER_EC
    n_blk, n_chunk = n // tm, e // ec
    n_steps = n_blk * n_chunk + 1
    return pl.pallas_call(
        functools.partial(_peer_kernel, n_chunk=n_chunk, n_steps=n_steps),
        out_shape=jax.ShapeDtypeStruct((n, d), F32),
        grid=(n_steps,),
        in_specs=[
            pl.BlockSpec((tm, d), lambda s: (jnp.minimum(s // n_chunk, n_blk - 1), 0)),
            pl.BlockSpec((d, d), lambda s: (0, 0)),
            pl.BlockSpec((PEER_HEADS, 2 * N_KEYS, LANES), lambda s: (0, 0, 0)),
            pl.BlockSpec((ec, d), lambda s: (s % n_chunk, 0)),
            pl.BlockSpec((d, ec), lambda s: (0, (s + n_chunk - 1) % n_chunk)),
        ],
        out_specs=pl.BlockSpec((tm, d), lambda s: (jnp.maximum(s - 1, 0) // n_chunk, 0)),
        scratch_shapes=[
            pltpu.VMEM((d, tm), BF16),
            pltpu.VMEM((d, tm), F32),
            pltpu.VMEM((8 * SC_PITCH, LANES), F32),
            pltpu.VMEM((8 * TB_PITCH, LANES), F32),
            pltpu.VMEM((8 * TB_PITCH, LANES), F32),
            pltpu.VMEM((8 * TB_PITCH, LANES), F32),
            pltpu.VMEM((8 * TB_PITCH, LANES), F32),
            pltpu.VMEM((PEER_HEADS, N_KEYS, tm), F32),
            pltpu.VMEM((PEER_HEADS, N_KEYS, tm), F32),
            pltpu.VMEM((PEER_HEADS, N_KEYS // PEER_PACK, PEER_PACK, tm), BF16),
            pltpu.VMEM((PEER_HEADS, N_KEYS // PEER_PACK, PEER_PACK, tm), BF16),
            pltpu.VMEM((ec, tm), BF16),
            pltpu.VMEM((ec, tm), BF16),
            pltpu.VMEM((d, tm), F32),
        ],
        compiler_params=_cparams(("arbitrary",)),
        name="peer",
    )(hp, wqt, kkt, u, vt)


def _resid_norm_kernel(x_ref, y_ref, m_ref, mn_ref, g_ref, xo_ref, hn_ref):
    xn = x_ref[0] + m_ref[0, 0, 5:6, :] * y_ref[0]
    xo_ref[0] = xn
    hn_ref[0] = _rms_mod(xn, g_ref[...], mn_ref[0, 0, 0:1, :], mn_ref[0, 0, 1:2, :]).astype(hn_ref.dtype)


def _resid_final_kernel(x_ref, y_ref, m_ref, g_ref, o_ref):
    xn = x_ref[0] + m_ref[0, 0, 5:6, :] * y_ref[0]
    o_ref[0] = xn * lax.rsqrt(jnp.mean(xn * xn, axis=-1, keepdims=True) + EPS) * g_ref[...]


def _resid_norm(x, y, mods, mods_next, g_next):
    b, t, d = x.shape
    tm = TOK_BLK
    xspec = pl.BlockSpec((1, tm, d), lambda bi, i: (bi, i, 0))
    mspec = pl.BlockSpec((1, 1, 6, d), lambda bi, i: (bi, jnp.minimum(i, 1), 0, 0))
    return pl.pallas_call(
        _resid_norm_kernel,
        out_shape=(jax.ShapeDtypeStruct((b, t, d), F32), jax.ShapeDtypeStruct((b, t, d), BF16)),
        grid=(b, t // tm),
        in_specs=[xspec, xspec, mspec, mspec, pl.BlockSpec((1, d), lambda bi, i: (0, 0))],
        out_specs=(xspec, xspec),
        compiler_params=_cparams(("parallel", "parallel")),
        name="resid_norm",
    )(x, y, mods, mods_next, g_next.reshape(1, d))


def _resid_final(x, y, mods, g):
    b, t, d = x.shape
    tm = TOK_BLK
    xspec = pl.BlockSpec((1, tm, d), lambda bi, i: (bi, i, 0))
    return pl.pallas_call(
        _resid_final_kernel,
        out_shape=jax.ShapeDtypeStruct((b, t, d), F32),
        grid=(b, t // tm),
        in_specs=[xspec, xspec, pl.BlockSpec((1, 1, 6, d), lambda bi, i: (bi, 1, 0, 0)),
                  pl.BlockSpec((1, d), lambda bi, i: (0, 0))],
        out_specs=xspec,
        compiler_params=_cparams(("parallel", "parallel")),
        name="resid_final",
    )(x, y, mods, g.reshape(1, d))


def _rope_tables(n_latent, ctx_len):
    rows = n_latent // GRID_W
    row, col = jnp.meshgrid(jnp.arange(rows), jnp.arange(GRID_W), indexing='ij')
    pos = jnp.stack([row.reshape(-1), col.reshape(-1)], axis=-1).astype(F32)
    inv = ROPE_THETA ** (-jnp.arange(ROPE_FREQS, dtype=F32) / ROPE_FREQS)
    ang = pos[:, :, None] * inv
    cos, sin = jnp.cos(ang), jnp.sin(ang)
    lane = jnp.arange(LANES)
    axis = (lane % ATTN_DH) // (2 * ROPE_FREQS)
    freq = lane % ROPE_FREQS
    second = (lane % (2 * ROPE_FREQS)) >= ROPE_FREQS
    c = cos[:, axis, freq]
    s = sin[:, axis, freq]
    sa = jnp.where(second, 0.0, -s)
    sb = jnp.where(second, s, 0.0)
    ones = jnp.ones((ctx_len, LANES), F32)
    zeros = jnp.zeros((ctx_len, LANES), F32)
    ck = jnp.concatenate([ones, c], axis=0)
    sak = jnp.concatenate([zeros, sa], axis=0)
    sbk = jnp.concatenate([zeros, sb], axis=0)
    scale = ATTN_DH ** -0.5 * math.log2(math.e)
    return ck * scale, sak * scale, sbk * scale, ck, sak, sbk


def _block_diag_pairs(w):
    lead = w.shape[:-3]
    w = w.reshape(lead + (LRU_BLOCKS // 2, 2, LRU_BW, LRU_BW))
    z = jnp.zeros_like(w[..., 0, :, :])
    top = jnp.concatenate([w[..., 0, :, :], z], axis=-1)
    bot = jnp.concatenate([z, w[..., 1, :, :]], axis=-1)
    return jnp.concatenate([top, bot], axis=-2)


def _peer_key_tiles(keys):
    z = jnp.zeros_like(keys[:, 0])
    top = jnp.concatenate([keys[:, 0], z], axis=-1)
    bot = jnp.concatenate([z, keys[:, 1]], axis=-1)
    return jnp.concatenate([top, bot], axis=1)


def kernel(x, c, ctx, c_ctx, mod_w, mod_b, norm1_g, norm2_g, w_in, conv_w, conv_b, lru_w, lru_b, lru_lam,
           diff_lam, subln_g, w_br_lru, w_br_attn, w_out, peer_wq, peer_keys, peer_u, peer_v, final_g):
    b, seq, d = x.shape
    ctx_len = ctx.shape[1]
    depth = mod_w.shape[0]
    t = ctx_len + seq

    cc = jnp.concatenate([c, c_ctx[None, :], jnp.zeros((16 - b - 1, d), F32)], axis=0)
    m = _mods(cc, mod_w, mod_b)
    m_lat = m[:, :b].reshape(depth, b, 1, 6, d)
    m_ctx = jnp.broadcast_to(m[:, b].reshape(depth, 1, 1, 6, d), (depth, b, 1, 6, d))
    mods = jnp.concatenate([m_ctx, m_lat], axis=2)

    tabs = _rope_tables(seq, ctx_len)
    xs = jnp.concatenate([ctx, x], axis=1)
    hn = _norm_mod(xs, norm1_g[0], mods[0])

    for li in range(depth):
        last = li == depth - 1
        lambda_init = 0.8 - 0.6 * math.exp(-0.3 * li)
        w_in_b = w_in[li].astype(BF16)
        u_tm = _matmul_time_major(hn, w_in_b, d, F32)
        p = _matmul(hn, w_in_b, BF16, col0=d)
        hsum = _lru(u_tm.reshape(t, b, d), conv_w[li], conv_b[li],
                    _block_diag_pairs(lru_w[li]).astype(BF16), lru_b[li], lru_lam[li], ctx_len)
        ao = _attention(p, tabs, diff_lam[li], subln_g[li], lambda_init, ctx_len, not last)
        xm, hp = _merge(hsum.reshape(t, b * d), p, ao, xs, mods[li], norm2_g[li],
                        w_br_lru[li].astype(BF16), w_br_attn[li].astype(BF16), w_out[li].astype(BF16),
                        ctx_len, not last)
        nt = xm.shape[1]
        y = _peer(hp.reshape(b * nt, d), peer_wq[li].T.astype(BF16),
                  _peer_key_tiles(peer_keys[li]).astype(BF16),
                  peer_u[li].astype(BF16), peer_v[li].T.astype(BF16)).reshape(b, nt, d)
        if last:
            return _resid_final(xm, y, mods[li], final_g)
        xs, hn = _resid_norm(xm, y, mods[li], mods[li + 1], norm1_g[li + 1])
```
